```python
import jax, jax.numpy as jnp
from jax import lax
import numpy as np

D_MODEL = 1024
BATCH = 16
SEQ = 2048
DEPTH = 2
DEC_BATCH = 32
DEC_SEQ = 8
PAST_LEN = 16384
PAGE_SIZE = 128

HEAD_DIM = 64
A_GROUPS = ((128, 1), (512, 4), (2048, 16))
N_A_GROUPS = 3
HEADS_PER_GROUP = 8
A_WIDTH = HEADS_PER_GROUP * HEAD_DIM
ROPE_THETA = 10000.0
BAND_BLOCK = 128
B_CHUNK = 128
B_WIDTH = 2 * D_MODEL
B_GROUPS = 16
B_GROUP_DIM = B_WIDTH // B_GROUPS
D_FF = 2816
N_MIXERS = 2
N_A_LAYERS = (DEPTH + 1) // 2
N_B_LAYERS = DEPTH // 2
EPS = 1e-6

kernel_name = "hybrid_dilated_attn_chunk_gmlp_macaron_step"


def rms_norm(x, g):
    xf = x.astype(jnp.float32)
    y = xf * lax.rsqrt(jnp.mean(xf * xf, axis=-1, keepdims=True) + EPS)
    return (y * g.astype(jnp.float32)).astype(x.dtype)


def rotary(x, pos):
    half = HEAD_DIM // 2
    inv = ROPE_THETA ** (-jnp.arange(half, dtype=jnp.float32) / half)
    ang = pos.astype(jnp.float32)[:, None] * inv[None, :]
    shape = (1, pos.shape[0]) + (1,) * (x.ndim - 3) + (half,)
    cos = jnp.cos(ang).reshape(shape)
    sin = jnp.sin(ang).reshape(shape)
    xf = x.astype(jnp.float32)
    x1, x2 = xf[..., :half], xf[..., half:]
    return jnp.concatenate([x1 * cos - x2 * sin, x2 * cos + x1 * sin], axis=-1).astype(x.dtype)


def swiglu_half_step(x, g, w_in, w_out):
    h = rms_norm(x, g) @ w_in
    gate, up = jnp.split(h, 2, axis=-1)
    return x + 0.5 * ((jax.nn.silu(gate) * up) @ w_out)


def masked_softmax_av(s, valid, v, eq):
    s = jnp.where(valid, s, -jnp.inf)
    m = jnp.max(s, axis=-1, keepdims=True)
    p = jnp.exp(s - m)
    l = jnp.sum(p, axis=-1, keepdims=True)
    o = jnp.einsum(eq, (p / l).astype(v.dtype), v)
    return o, (m + jnp.log(l))[..., 0]


def a_project(xn, w_qkv, qk_gain, pos):
    B, T, _ = xn.shape
    qkv = (xn @ w_qkv).reshape(B, T, 3, N_A_GROUPS, HEADS_PER_GROUP, HEAD_DIM)
    q = rotary(rms_norm(qkv[:, :, 0], qk_gain[0]), pos)
    k = rotary(rms_norm(qkv[:, :, 1], qk_gain[1]), pos)
    return q, k, qkv[:, :, 2]


def dilated_band_attention(q, k, v, window, dilation):
    B, S, H, Dh = q.shape
    n_back = window // dilation
    L = S // dilation
    nblk = -(-L // BAND_BLOCK)
    Lp = nblk * BAND_BLOCK

    def by_residue(a, front):
        a = a.reshape(B, L, dilation, H, Dh).transpose(0, 2, 1, 3, 4)
        return jnp.pad(a, ((0, 0), (0, 0), (front, Lp - L), (0, 0), (0, 0)))

    qb = by_residue(q, 0).reshape(B, dilation, nblk, BAND_BLOCK, H, Dh)
    kb = by_residue(k, BAND_BLOCK).reshape(B, dilation, nblk + 1, BAND_BLOCK, H, Dh)
    vb = by_residue(v, BAND_BLOCK).reshape(B, dilation, nblk + 1, BAND_BLOCK, H, Dh)
    kw = jnp.concatenate([kb[:, :, :-1], kb[:, :, 1:]], axis=3)
    vw = jnp.concatenate([vb[:, :, :-1], vb[:, :, 1:]], axis=3)
    s = jnp.einsum('brnqhd,brnkhd->brnhqk', qb, kw,
                   preferred_element_type=jnp.float32) * (HEAD_DIM ** -0.5)
    jq = jnp.arange(nblk)[:, None] * BAND_BLOCK + jnp.arange(BAND_BLOCK)[None, :]
    jk = (jnp.arange(nblk)[:, None] - 1) * BAND_BLOCK + jnp.arange(2 * BAND_BLOCK)[None, :]
    dist = jq[:, :, None] - jk[:, None, :]
    valid = (dist >= 0) & (dist <= n_back) & (jk[:, None, :] >= 0)
    o, lse = masked_softmax_av(s, valid[None, None, :, None], vw, 'brnhqk,brnkhd->brnqhd')
    o = o.reshape(B, dilation, Lp, H, Dh)[:, :, :L].transpose(0, 2, 1, 3, 4).reshape(B, S, H, Dh)
    lse = lse.transpose(0, 1, 2, 4, 3).reshape(B, dilation, Lp, H)[:, :, :L]
    lse = lse.transpose(0, 2, 1, 3).reshape(B, S, H)
    return o, lse


def dilated_gather_attention(q, k_all, v_all, window, dilation, L):
    T = q.shape[1]
    n_back = window // dilation
    idx = L + jnp.arange(T)[:, None] - dilation * jnp.arange(n_back + 1)[None, :]
    valid = idx >= 0
    idxc = jnp.maximum(idx, 0)
    kg = k_all[:, idxc]
    vg = v_all[:, idxc]
    s = jnp.einsum('bthd,btkhd->bthk', q, kg,
                   preferred_element_type=jnp.float32) * (HEAD_DIM ** -0.5)
    return masked_softmax_av(s, valid[None, :, None, :], vg, 'bthk,btkhd->bthd')


def merge_groups(outs, lses):
    o = jnp.stack(outs, axis=0)
    w = jax.nn.softmax(jnp.stack(lses, axis=0), axis=0)
    return jnp.sum(w[..., None] * o.astype(jnp.float32), axis=0).astype(o.dtype)


def mixer_a_prompt(xn, w_qkv, qk_gain, w_o):
    B, S, _ = xn.shape
    q, k, v = a_project(xn, w_qkv, qk_gain, jnp.arange(S))
    outs, lses, tails = [], [], []
    for g, (window, dilation) in enumerate(A_GROUPS):
        o, lse = dilated_band_attention(q[:, :, g], k[:, :, g], v[:, :, g], window, dilation)
        outs.append(o)
        lses.append(lse)
        keep = min(window, S)
        tails.append(jnp.stack([k[:, S - keep:, g], v[:, S - keep:, g]], axis=2))
    y = merge_groups(outs, lses).reshape(B, S, A_WIDTH) @ w_o
    return y, tails


def mixer_a_sample(xn, bufs, w_qkv, qk_gain, w_o):
    B, T, _ = xn.shape
    q, k, v = a_project(xn, w_qkv, qk_gain, PAST_LEN + jnp.arange(T))
    outs, lses, rows = [], [], []
    for g, (window, dilation) in enumerate(A_GROUPS):
        buf = bufs[g]
        L = buf.shape[1]
        k_all = jnp.concatenate([buf[:, :, 0], k[:, :, g]], axis=1)
        v_all = jnp.concatenate([buf[:, :, 1], v[:, :, g]], axis=1)
        o, lse = dilated_gather_attention(q[:, :, g], k_all, v_all, window, dilation, L)
        outs.append(o)
        lses.append(lse)
        rows.append(jnp.stack([k[:, :, g], v[:, :, g]], axis=2))
    y = merge_groups(outs, lses).reshape(B, T, A_WIDTH) @ w_o
    return y, rows


def chunk_spatial_gating(xn, w_uv, v_gain, w_sp, b_sp, w_o):
    B, T, _ = xn.shape
    u, v = jnp.split(jax.nn.gelu(xn @ w_uv), 2, axis=-1)
    v = rms_norm(v, v_gain)
    nch = -(-T // B_CHUNK)
    Tp = nch * B_CHUNK
    vc = jnp.pad(v, ((0, 0), (0, Tp - T), (0, 0))).reshape(B, nch, B_CHUNK, B_GROUPS, B_GROUP_DIM)
    causal = jnp.tril(jnp.ones((B_CHUNK, B_CHUNK), dtype=bool))
    w = jnp.where(causal[None], w_sp, jnp.zeros_like(w_sp))
    mixed = jnp.einsum('gts,bnsgc->bntgc', w, vc) + b_sp.T[None, None, :, :, None]
    mixed = mixed.reshape(B, Tp, B_WIDTH)[:, :T]
    return (u * mixed) @ w_o, v


def setup_inputs(seed: int = 0) -> dict:
    key = jax.random.key(seed)
    ks = jax.random.split(key, 18)
    f32 = jnp.float32
    d_qkv = 3 * N_A_GROUPS * HEADS_PER_GROUP * HEAD_DIM

    def cache(k, window):
        return jax.random.normal(k, (N_A_LAYERS, DEC_BATCH, min(window, PAST_LEN), 2,
                                     HEADS_PER_GROUP, HEAD_DIM), f32)

    return {
        "x_prompt": jax.random.normal(ks[0], (BATCH, SEQ, D_MODEL), f32),
        "x_sample": jax.random.normal(ks[1], (DEC_BATCH, DEC_SEQ, D_MODEL), f32),
        "cache_kv_w128": cache(ks[2], A_GROUPS[0][0]),
        "cache_kv_w512": cache(ks[3], A_GROUPS[1][0]),
        "cache_kv_w2048": cache(ks[4], A_GROUPS[2][0]),
        "norm_g": 1.0 + 0.02 * jax.random.normal(ks[5], (DEPTH, 3, D_MODEL), f32),
        "w_ffn_in": jax.random.normal(ks[6], (DEPTH, 2, D_MODEL, 2 * D_FF), f32) * D_MODEL ** -0.5,
        "w_ffn_out": jax.random.normal(ks[7], (DEPTH, 2, D_FF, D_MODEL), f32) * D_FF ** -0.5,
        "w_qkv_a": jax.random.normal(ks[8], (N_A_LAYERS, D_MODEL, d_qkv), f32) * D_MODEL ** -0.5,
        "qk_gain_a": 1.0 + 0.02 * jax.random.normal(ks[9], (N_A_LAYERS, 2, HEAD_DIM), f32),
        "w_o_a": jax.random.normal(ks[10], (N_A_LAYERS, A_WIDTH, D_MODEL), f32) * A_WIDTH ** -0.5,
        "w_uv_b": jax.random.normal(ks[11], (N_B_LAYERS, D_MODEL, 2 * B_WIDTH), f32) * D_MODEL ** -0.5,
        "v_gain_b": 1.0 + 0.02 * jax.random.normal(ks[12], (N_B_LAYERS, B_WIDTH), f32),
        "w_sp_b": jax.random.normal(ks[13], (N_B_LAYERS, B_GROUPS, B_CHUNK, B_CHUNK), f32) * B_CHUNK ** -0.5,
        "b_sp_b": 1.0 + 0.02 * jax.random.normal(ks[14], (N_B_LAYERS, B_GROUPS, B_CHUNK), f32),
        "w_o_b": jax.random.normal(ks[15], (N_B_LAYERS, B_WIDTH, D_MODEL), f32) * B_WIDTH ** -0.5,
    }


def reference(x_prompt, x_sample, cache_kv_w128, cache_kv_w512, cache_kv_w2048, norm_g,
              w_ffn_in, w_ffn_out, w_qkv_a, qk_gain_a, w_o_a, w_uv_b, v_gain_b, w_sp_b, b_sp_b, w_o_b):
    xp, xs = x_prompt, x_sample
    tails_p = [[], [], []]
    rows_s = [[], [], []]
    v_rows_s = []
    for i in range(DEPTH):
        j = i // N_MIXERS
        xp = swiglu_half_step(xp, norm_g[i, 0], w_ffn_in[i, 0], w_ffn_out[i, 0])
        xs = swiglu_half_step(xs, norm_g[i, 0], w_ffn_in[i, 0], w_ffn_out[i, 0])
        xpn = rms_norm(xp, norm_g[i, 1])
        xsn = rms_norm(xs, norm_g[i, 1])
        if i % N_MIXERS == 0:
            yp, tails = mixer_a_prompt(xpn, w_qkv_a[j], qk_gain_a[j], w_o_a[j])
            ys, rows = mixer_a_sample(xsn, (cache_kv_w128[j], cache_kv_w512[j], cache_kv_w2048[j]),
                                      w_qkv_a[j], qk_gain_a[j], w_o_a[j])
            for g in range(N_A_GROUPS):
                tails_p[g].append(tails[g])
                rows_s[g].append(rows[g])
        else:
            yp, _ = chunk_spatial_gating(xpn, w_uv_b[j], v_gain_b[j], w_sp_b[j], b_sp_b[j], w_o_b[j])
            ys, v_new = chunk_spatial_gating(xsn, w_uv_b[j], v_gain_b[j], w_sp_b[j], b_sp_b[j], w_o_b[j])
            v_rows_s.append(v_new)
        xp = xp + yp
        xs = xs + ys
        xp = swiglu_half_step(xp, norm_g[i, 2], w_ffn_in[i, 1], w_ffn_out[i, 1])
        xs = swiglu_half_step(xs, norm_g[i, 2], w_ffn_in[i, 1], w_ffn_out[i, 1])
    return (xp, xs,
            jnp.stack(tails_p[0], axis=0), jnp.stack(tails_p[1], axis=0), jnp.stack(tails_p[2], axis=0),
            jnp.stack(rows_s[0], axis=0), jnp.stack(rows_s[1], axis=0), jnp.stack(rows_s[2], axis=0),
            jnp.stack(v_rows_s, axis=0))
```

```python
import functools

import jax
import jax.numpy as jnp
from jax import lax
from jax.experimental import pallas as pl
from jax.experimental.pallas import tpu as pltpu

F32 = jnp.float32
BF16 = jnp.bfloat16

D_MODEL = 1024
D_FF = 2816
HEAD_DIM = 64
HEADS = 8
GROUP_W = HEADS * HEAD_DIM
QKV_W = 3 * GROUP_W
A_GROUPS = ((128, 1), (512, 4), (2048, 16))
N_BACK = 128
ROPE_THETA = 10000.0
B_WIDTH = 2048
B_GROUPS = 16
B_CHUNK = 128
EPS = 1e-6
NEG = -1e30

SEQ = 2048
BATCH = 16
DEC_BATCH = 32
DEC_SEQ = 8
PAST_LEN = 16384

V7X_VMEM_LIMIT_BYTES = 56 * 1024 * 1024


def _dot(a, b):
    return jnp.dot(a, b, preferred_element_type=F32)


def _dot_nt(a, b):
    return lax.dot_general(a, b, (((1,), (1,)), ((), ())), preferred_element_type=F32)


def _rms(x, g):
    ms = jnp.mean(x * x, axis=-1, keepdims=True)
    return x * lax.rsqrt(ms + EPS) * g


def _gelu_tanh(x):
    cdf = 0.5 * (1.0 + jnp.tanh(0.7978845608028654 * (x + 0.044715 * (x * x * x))))
    return x * cdf


def _params(n_axes):
    return pltpu.CompilerParams(
        dimension_semantics=("arbitrary",) * n_axes,
        vmem_limit_bytes=V7X_VMEM_LIMIT_BYTES,
    )


def _resident(shape):
    nd = len(shape)
    return pl.BlockSpec(shape, lambda *_: (0,) * nd, pipeline_mode=pl.Buffered(1))


FF_CHUNK = 256


def _ffn_kernel(x_ref, g_ref, win_ref, wout_ref, o_ref, h_ref):
    x = x_ref[...]
    xn = _rms(x, g_ref[...]).astype(BF16)
    for c in range(D_FF // FF_CHUNK):
        lo = c * FF_CHUNK
        gate = _dot(xn, win_ref[:, lo:lo + FF_CHUNK])
        up = _dot(xn, win_ref[:, D_FF + lo:D_FF + lo + FF_CHUNK])
        act = gate * (1.0 / (1.0 + jnp.exp(-gate)))
        h_ref[:, lo:lo + FF_CHUNK] = (act * up).astype(BF16)
    o_ref[...] = x + 0.5 * _dot(h_ref[...], wout_ref[...])


def _ffn(x, g, w_in, w_out, tm):
    n = x.shape[0]
    return pl.pallas_call(
        _ffn_kernel,
        out_shape=jax.ShapeDtypeStruct((n, D_MODEL), F32),
        grid=(n // tm,),
        in_specs=[
            pl.BlockSpec((tm, D_MODEL), lambda i: (i, 0)),
            _resident((1, D_MODEL)),
            _resident((D_MODEL, 2 * D_FF)),
            _resident((D_FF, D_MODEL)),
        ],
        out_specs=pl.BlockSpec((tm, D_MODEL), lambda i: (i, 0)),
        scratch_shapes=[pltpu.VMEM((tm, D_FF), BF16)],
        compiler_params=_params(1),
        name="ffn",
    )(x, g, w_in, w_out)


QK_BLOCK = 256


def _qkv_kernel(x_ref, g_ref, w_ref, gain_ref, cos_ref, sin_ref, bd_ref,
                q_ref, kv0_ref, kv1_ref, kv2_ref):
    tm = x_ref.shape[0]
    xn = _rms(x_ref[...], g_ref[...]).astype(BF16)
    cos = cos_ref[...]
    sin = sin_ref[...]
    bd = bd_ref[...]
    lane = lax.broadcasted_iota(jnp.int32, (tm, QK_BLOCK), 1)
    first_half = (lane & (HEAD_DIM - 1)) < (HEAD_DIM // 2)

    def norm_rot(y, gain):
        ss = y * y
        hi = ss.astype(BF16)
        lo = (ss - hi.astype(F32)).astype(BF16)
        ms = _dot(hi, bd) + _dot(lo, bd)
        yn = y * lax.rsqrt(ms + EPS) * gain
        partner = jnp.where(first_half,
                            pltpu.roll(yn, QK_BLOCK - HEAD_DIM // 2, 1),
                            pltpu.roll(yn, HEAD_DIM // 2, 1))
        return yn * cos + partner * sin

    kv_refs = (kv0_ref, kv1_ref, kv2_ref)
    for g in range(len(A_GROUPS)):
        for half in range(GROUP_W // QK_BLOCK):
            c0 = g * GROUP_W + half * QK_BLOCK
            o0 = half * QK_BLOCK
            yq = _dot(xn, w_ref[:, c0:c0 + QK_BLOCK])
            q_ref[:, c0:c0 + QK_BLOCK] = norm_rot(yq, gain_ref[0:1, :])
            yk = _dot(xn, w_ref[:, QKV_W + c0:QKV_W + c0 + QK_BLOCK])
            kv_refs[g][:, o0:o0 + QK_BLOCK] = norm_rot(yk, gain_ref[1:2, :])
            kv_refs[g][:, GROUP_W + o0:GROUP_W + o0 + QK_BLOCK] = _dot(
                xn, w_ref[:, 2 * QKV_W + c0:2 * QKV_W + c0 + QK_BLOCK])


def _qkv(x, g, w, gain, cos, sin, bd, tm):
    n = x.shape[0]
    pos_blocks = cos.shape[0] // tm
    kv_shape = jax.ShapeDtypeStruct((n, 2 * GROUP_W), F32)
    return pl.pallas_call(
        _qkv_kernel,
        out_shape=(jax.ShapeDtypeStruct((n, QKV_W), F32), kv_shape, kv_shape, kv_shape),
        grid=(n // tm,),
        in_specs=[
            pl.BlockSpec((tm, D_MODEL), lambda i: (i, 0)),
            _resident((1, D_MODEL)),
            _resident((D_MODEL, 3 * QKV_W)),
            _resident((2, QK_BLOCK)),
            pl.BlockSpec((tm, QK_BLOCK), lambda i: (i % pos_blocks, 0)),
            pl.BlockSpec((tm, QK_BLOCK), lambda i: (i % pos_blocks, 0)),
            _resident((QK_BLOCK, QK_BLOCK)),
        ],
        out_specs=(
            pl.BlockSpec((tm, QKV_W), lambda i: (i, 0)),
            pl.BlockSpec((tm, 2 * GROUP_W), lambda i: (i, 0)),
            pl.BlockSpec((tm, 2 * GROUP_W), lambda i: (i, 0)),
            pl.BlockSpec((tm, 2 * GROUP_W), lambda i: (i, 0)),
        ),
        compiler_params=_params(1),
        name="qkv",
    )(x, g, w, gain, cos, sin, bd)


BLK = 128
PAIR_W = 128
N_STEPS = SEQ // BLK


def _attn_kernel(q0_ref, q1_ref, q2_ref, k0_ref, v0_ref, k1_ref, v1_ref, k2_ref, v2_ref,
                 o_ref, og0, og1, og2, lg0, lg1, lg2):
    q_refs = (q0_ref, q1_ref, q2_ref)
    k_refs = (k0_ref, k1_ref, k2_ref)
    v_refs = (v0_ref, v1_ref, v2_ref)
    og_refs = (og0, og1, og2)
    lg_refs = (lg0, lg1, lg2)

    lane = lax.broadcasted_iota(jnp.int32, (BLK, PAIR_W), 1)
    head_a = lane < HEAD_DIM
    row2 = lax.broadcasted_iota(jnp.int32, (2 * BLK, PAIR_W), 0)
    lane2 = lax.broadcasted_iota(jnp.int32, (2 * BLK, PAIR_W), 1)
    q_keep = (row2 >> 7) == (lane2 >> 6)
    iq = lax.broadcasted_iota(jnp.int32, (2 * BLK, 2 * BLK), 0) & (BLK - 1)
    col = lax.broadcasted_iota(jnp.int32, (2 * BLK, 2 * BLK), 1)
    is_cur = col >= BLK
    is_prev = col < BLK
    jk = col & (BLK - 1)

    for g, (_, dil) in enumerate(A_GROUPS):
        shift = dil.bit_length() - 1
        q_ref, k_ref, v_ref = q_refs[g], k_refs[g], v_refs[g]
        og_ref, lg_ref = og_refs[g], lg_refs[g]

        def body(c, carry, dil=dil, shift=shift, q_ref=q_ref, k_ref=k_ref, v_ref=v_ref,
                 og_ref=og_ref, lg_ref=lg_ref):
            r = c & (dil - 1)
            n = c >> shift
            start = r + n * (dil * BLK)
            pstart = jnp.maximum(start - dil * BLK, r)
            cur = pl.ds(start, BLK, stride=dil)
            prev = pl.ds(pstart, BLK, stride=dil)
            q = q_ref[cur, :] * (HEAD_DIM ** -0.5)
            q2 = jnp.where(q_keep, jnp.concatenate([q, q], axis=0), 0.0).astype(BF16)
            kcat = jnp.concatenate([k_ref[prev, :], k_ref[cur, :]], axis=0).astype(BF16)
            vcat = jnp.concatenate([v_ref[prev, :], v_ref[cur, :]], axis=0).astype(BF16)
            s = _dot_nt(q2, kcat)
            prev_off = jnp.where(n > 0, 0, 2 * BLK)
            valid = (is_cur & (jk <= iq)) | (is_prev & (jk >= iq + prev_off))
            s = jnp.where(valid, s, NEG)
            m = jnp.max(s, axis=1, keepdims=True)
            p = jnp.exp(s - m)
            l = jnp.sum(p, axis=1, keepdims=True)
            o = _dot(p.astype(BF16), vcat) / l
            lse = m + jnp.log(l)
            og_ref[cur, :] = jnp.where(head_a, o[:BLK], o[BLK:])
            lg_ref[cur, :] = jnp.where(head_a, lse[:BLK], lse[BLK:])
            return carry

        lax.fori_loop(0, N_STEPS, body, 0)

    def merge(i, carry):
        rows = pl.ds(pl.multiple_of(i * BLK, BLK), BLK)
        l0, l1, l2 = lg0[rows, :], lg1[rows, :], lg2[rows, :]
        mx = jnp.maximum(jnp.maximum(l0, l1), l2)
        e0, e1, e2 = jnp.exp(l0 - mx), jnp.exp(l1 - mx), jnp.exp(l2 - mx)
        num = e0 * og0[rows, :] + e1 * og1[rows, :] + e2 * og2[rows, :]
        o_ref[rows, :] = (num / (e0 + e1 + e2)).astype(o_ref.dtype)
        return carry

    lax.fori_loop(0, N_STEPS, merge, 0)


def _attn_prompt(q, kv0, kv1, kv2):
    n = q.shape[0]
    pairs = GROUP_W // PAIR_W
    blk = (SEQ, PAIR_W)
    q_specs = [pl.BlockSpec(blk, functools.partial(lambda b, hp, g: (b, g * pairs + hp), g=g))
               for g in range(3)]
    k_spec = pl.BlockSpec(blk, lambda b, hp: (b, hp))
    v_spec = pl.BlockSpec(blk, lambda b, hp: (b, pairs + hp))
    scratch = [pltpu.VMEM(blk, F32) for _ in range(6)]
    return pl.pallas_call(
        _attn_kernel,
        out_shape=jax.ShapeDtypeStruct((n, GROUP_W), BF16),
        grid=(n // SEQ, pairs),
        in_specs=q_specs + [k_spec, v_spec, k_spec, v_spec, k_spec, v_spec],
        out_specs=pl.BlockSpec(blk, lambda b, hp: (b, hp)),
        scratch_shapes=scratch,
        compiler_params=_params(2),
        name="attn_prompt",
    )(q, q, q, kv0, kv0, kv1, kv1, kv2, kv2)


NEW_PAD = 128
QROWS = HEADS * DEC_SEQ


def _attn_sample_kernel(q_ref, n0_ref, n1_ref, n2_ref, c0_ref, c1_ref, c2_ref, o_ref):
    new_refs = (n0_ref, n1_ref, n2_ref)
    cache_refs = (c0_ref, c1_ref, c2_ref)
    rowq = lax.broadcasted_iota(jnp.int32, (QROWS, GROUP_W), 0)
    laneq = lax.broadcasted_iota(jnp.int32, (QROWS, GROUP_W), 1)
    own_head = (rowq >> 3) == (laneq >> 6)
    outs, lses = [], []
    for g, (window, dil) in enumerate(A_GROUPS):
        q = q_ref[:, g * GROUP_W:(g + 1) * GROUP_W] * (HEAD_DIM ** -0.5)
        qbd = jnp.where(own_head, jnp.concatenate([q] * HEADS, axis=0), 0.0).astype(BF16)
        if g == 2:
            cache = c2_ref[...].reshape(window // 2, 2 * GROUP_W)
        else:
            cache = cache_refs[g][...]
        rows = cache.shape[0]
        new = jnp.concatenate(
            [new_refs[g][...], jnp.zeros((NEW_PAD - DEC_SEQ, 2 * GROUP_W), F32)], axis=0)
        kcat = jnp.concatenate([cache[:, :GROUP_W], new[:, :GROUP_W]], axis=0).astype(BF16)
        vcat = jnp.concatenate([cache[:, GROUP_W:], new[:, GROUP_W:]], axis=0).astype(BF16)
        width = rows + NEW_PAD
        col = lax.broadcasted_iota(jnp.int32, (QROWS, width), 1)
        j = lax.broadcasted_iota(jnp.int32, (QROWS, width), 0) & (DEC_SEQ - 1)
        if g == 2:
            key = ((col >> 3) << 4) + (col & 7)
        else:
            key = col
        back = window + j - key
        valid_old = ((back & (dil - 1)) == 0) & (key >= j)
        jn = col - rows
        valid_new = (jn <= j) & (((j - jn) & (dil - 1)) == 0)
        valid = ((col < rows) & valid_old) | ((col >= rows) & valid_new)
        s = jnp.where(valid, _dot_nt(qbd, kcat), NEG)
        m = jnp.max(s, axis=1, keepdims=True)
        p = jnp.exp(s - m)
        l = jnp.sum(p, axis=1, keepdims=True)
        o = _dot(p.astype(BF16), vcat) / l
        lse = m + jnp.log(l)
        om = jnp.where(own_head, o, 0.0)
        lm = jnp.where(own_head, lse, 0.0)
        og, lg = om[0:DEC_SEQ], lm[0:DEC_SEQ]
        for h in range(1, HEADS):
            og = og + om[h * DEC_SEQ:(h + 1) * DEC_SEQ]
            lg = lg + lm[h * DEC_SEQ:(h + 1) * DEC_SEQ]
        outs.append(og)
        lses.append(lg)
    mx = jnp.maximum(jnp.maximum(lses[0], lses[1]), lses[2])
    e = [jnp.exp(x - mx) for x in lses]
    num = e[0] * outs[0] + e[1] * outs[1] + e[2] * outs[2]
    o_ref[...] = num / (e[0] + e[1] + e[2])


def _attn_sample(q, new0, new1, new2, c0, c1, c2):
    w2 = A_GROUPS[2][0]
    d2 = A_GROUPS[2][1]
    c2v = c2.reshape(DEC_BATCH, w2 // d2, d2, 2 * GROUP_W)
    row_spec = lambda w: pl.BlockSpec((DEC_SEQ, w), lambda b: (b, 0))
    return pl.pallas_call(
        _attn_sample_kernel,
        out_shape=jax.ShapeDtypeStruct((DEC_BATCH * DEC_SEQ, GROUP_W), F32),
        grid=(DEC_BATCH,),
        in_specs=[
            row_spec(QKV_W), row_spec(2 * GROUP_W), row_spec(2 * GROUP_W), row_spec(2 * GROUP_W),
            pl.BlockSpec((None, A_GROUPS[0][0], 2 * GROUP_W), lambda b: (b, 0, 0)),
            pl.BlockSpec((None, A_GROUPS[1][0], 2 * GROUP_W), lambda b: (b, 0, 0)),
            pl.BlockSpec((None, w2 // d2, DEC_SEQ, 2 * GROUP_W), lambda b: (b, 0, 0, 0)),
        ],
        out_specs=row_spec(GROUP_W),
        compiler_params=_params(1),
        name="attn_sample",
    )(q, new0, new1, new2, c0, c1, c2v)


def _proj_kernel(x_ref, m_ref, w_ref, o_ref):
    o_ref[...] = x_ref[...] + _dot(m_ref[...].astype(BF16), w_ref[...])


def _proj(x, m, w, tm):
    n = x.shape[0]
    return pl.pallas_call(
        _proj_kernel,
        out_shape=jax.ShapeDtypeStruct((n, D_MODEL), F32),
        grid=(n // tm,),
        in_specs=[
            pl.BlockSpec((tm, D_MODEL), lambda i: (i, 0)),
            pl.BlockSpec((tm, GROUP_W), lambda i: (i, 0)),
            _resident((GROUP_W, D_MODEL)),
        ],
        out_specs=pl.BlockSpec((tm, D_MODEL), lambda i: (i, 0)),
        compiler_params=_params(1),
        name="attn_out_proj",
    )(x, m, w)


UV_CHUNK = 512
B_GROUP_DIM = B_WIDTH // B_GROUPS


def _mixb_kernel(x_ref, g_ref, wuv_ref, vgain_ref, wsp_ref, bt_ref, wo_ref, *rest,
                 chunk, emit_v):
    if emit_v:
        o_ref, vout_ref, u_ref, v_ref, h_ref = rest
    else:
        o_ref, u_ref, v_ref, h_ref = rest
    tm = x_ref.shape[0]
    r = wsp_ref.shape[1]
    nsub = tm // r
    x = x_ref[...]
    xn = _rms(x, g_ref[...]).astype(BF16)
    ssq = jnp.zeros((tm, 1), F32)
    for c in range(B_WIDTH // UV_CHUNK):
        lo = c * UV_CHUNK
        u_ref[:, lo:lo + UV_CHUNK] = _gelu_tanh(_dot(xn, wuv_ref[:, lo:lo + UV_CHUNK]))
        vv = _gelu_tanh(_dot(xn, wuv_ref[:, B_WIDTH + lo:B_WIDTH + lo + UV_CHUNK]))
        v_ref[:, lo:lo + UV_CHUNK] = vv
        ssq = ssq + jnp.sum(vv * vv, axis=-1, keepdims=True)
    inv = lax.rsqrt(ssq * (1.0 / B_WIDTH) + EPS)
    row = lax.broadcasted_iota(jnp.int32, (r, r), 0)
    col = lax.broadcasted_iota(jnp.int32, (r, r), 1)
    shift = chunk.bit_length() - 1
    keep = ((row >> shift) == (col >> shift)) & ((col & (chunk - 1)) <= (row & (chunk - 1)))
    for g in range(B_GROUPS):
        lanes = slice(g * B_GROUP_DIM, (g + 1) * B_GROUP_DIM)
        vg = v_ref[:, lanes] * inv * vgain_ref[:, lanes]
        if emit_v:
            vout_ref[:, lanes] = vg
        vb = vg.astype(BF16)
        if nsub > 1:
            vb = jnp.concatenate([vb[s * r:(s + 1) * r] for s in range(nsub)], axis=1)
        w = jnp.where(keep, wsp_ref[g], 0.0).astype(BF16)
        mixed = _dot(w, vb) + bt_ref[:, g:g + 1]
        for s in range(nsub):
            rows = slice(s * r, (s + 1) * r)
            h_ref[rows, lanes] = (
                u_ref[rows, lanes] * mixed[:, s * B_GROUP_DIM:(s + 1) * B_GROUP_DIM]
            ).astype(BF16)
    o_ref[...] = x + _dot(h_ref[...], wo_ref[...])


def _mixb(x, g, w_uv, v_gain, w_sp, b_t, w_o, tm, chunk, emit_v):
    n = x.shape[0]
    r = w_sp.shape[1]
    x_spec = pl.BlockSpec((tm, D_MODEL), lambda i: (i, 0))
    out_shape = [jax.ShapeDtypeStruct((n, D_MODEL), F32)]
    out_specs = [x_spec]
    if emit_v:
        out_shape.append(jax.ShapeDtypeStruct((n, B_WIDTH), F32))
        out_specs.append(pl.BlockSpec((tm, B_WIDTH), lambda i: (i, 0)))
    return pl.pallas_call(
        functools.partial(_mixb_kernel, chunk=chunk, emit_v=emit_v),
        out_shape=tuple(out_shape),
        grid=(n // tm,),
        in_specs=[
            x_spec,
            _resident((1, D_MODEL)),
            _resident((D_MODEL, 2 * B_WIDTH)),
            _resident((1, B_WIDTH)),
            _resident((B_GROUPS, r, r)),
            _resident((r, B_GROUPS)),
            _resident((B_WIDTH, D_MODEL)),
        ],
        out_specs=tuple(out_specs),
        scratch_shapes=[
            pltpu.VMEM((tm, B_WIDTH), F32),
            pltpu.VMEM((tm, B_WIDTH), F32),
            pltpu.VMEM((tm, B_WIDTH), BF16),
        ],
        compiler_params=_params(1),
        name="spatial_gating",
    )(x, g, w_uv, v_gain, w_sp, b_t, w_o)


def _rope_tables(pos):
    half = HEAD_DIM // 2
    inv = ROPE_THETA ** (-jnp.arange(half, dtype=F32) / half)
    ang = pos.astype(F32)[:, None] * inv[None, :]
    cos = jnp.cos(ang)
    sin = jnp.sin(ang)
    reps = QK_BLOCK // HEAD_DIM
    cos_t = jnp.tile(jnp.concatenate([cos, cos], axis=-1), (1, reps))
    sin_t = jnp.tile(jnp.concatenate([-sin, sin], axis=-1), (1, reps))
    return cos_t, sin_t


def kernel(x_prompt, x_sample, cache_kv_w128, cache_kv_w512, cache_kv_w2048, norm_g,
           w_ffn_in, w_ffn_out, w_qkv_a, qk_gain_a, w_o_a, w_uv_b, v_gain_b, w_sp_b, b_sp_b,
           w_o_b):
    xp = x_prompt.reshape(BATCH * SEQ, D_MODEL)
    xs = x_sample.reshape(DEC_BATCH * DEC_SEQ, D_MODEL)
    n_s = DEC_BATCH * DEC_SEQ
    tm_p = 1024

    w_in = w_ffn_in.astype(BF16)
    w_out = w_ffn_out.astype(BF16)
    w_qkv = w_qkv_a.astype(BF16)
    w_oa = w_o_a.astype(BF16)
    w_uv = w_uv_b.astype(BF16)
    w_ob = w_o_b.astype(BF16)
    gn = norm_g.reshape(2, 3, 1, D_MODEL)

    head_id = jnp.arange(QK_BLOCK) // HEAD_DIM
    bd = jnp.where(head_id[:, None] == head_id[None, :], 1.0 / HEAD_DIM, 0.0).astype(BF16)
    gain = jnp.tile(qk_gain_a[0], (1, QK_BLOCK // HEAD_DIM))
    cos_p, sin_p = _rope_tables(jnp.arange(SEQ))
    cos_s, sin_s = _rope_tables(PAST_LEN + jnp.arange(DEC_SEQ))
    cos_s = jnp.tile(cos_s, (DEC_BATCH, 1))
    sin_s = jnp.tile(sin_s, (DEC_BATCH, 1))

    xp = _ffn(xp, gn[0, 0], w_in[0, 0], w_out[0, 0], tm_p)
    xs = _ffn(xs, gn[0, 0], w_in[0, 0], w_out[0, 0], n_s)

    qp, kvp0, kvp1, kvp2 = _qkv(xp, gn[0, 1], w_qkv[0], gain, cos_p, sin_p, bd, 512)
    qs, kvs0, kvs1, kvs2 = _qkv(xs, gn[0, 1], w_qkv[0], gain, cos_s, sin_s, bd, n_s)

    mp = _attn_prompt(qp, kvp0, kvp1, kvp2)
    kv_cols = 2 * GROUP_W
    ms = _attn_sample(
        qs, kvs0, kvs1, kvs2,
        cache_kv_w128[0].reshape(DEC_BATCH, A_GROUPS[0][0], kv_cols),
        cache_kv_w512[0].reshape(DEC_BATCH, A_GROUPS[1][0], kv_cols),
        cache_kv_w2048[0].reshape(DEC_BATCH, A_GROUPS[2][0], kv_cols))

    xp = _proj(xp, mp, w_oa[0], tm_p)
    xs = _proj(xs, ms, w_oa[0], n_s)

    xp = _ffn(xp, gn[0, 2], w_in[0, 1], w_out[0, 1], tm_p)
    xs = _ffn(xs, gn[0, 2], w_in[0, 1], w_out[0, 1], n_s)

    xp = _ffn(xp, gn[1, 0], w_in[1, 0], w_out[1, 0], tm_p)
    xs = _ffn(xs, gn[1, 0], w_in[1, 0], w_out[1, 0], n_s)

    vgain = v_gain_b[0].reshape(1, B_WIDTH)
    (xp,) = _mixb(xp, gn[1, 1], w_uv[0], vgain, w_sp_b[0], b_sp_b[0].T, w_ob[0],
                  tm=512, chunk=B_CHUNK, emit_v=False)
    w_sp_s = jnp.tile(w_sp_b[0][:, :DEC_SEQ, :DEC_SEQ], (1, DEC_BATCH, DEC_BATCH))
    b_t_s = jnp.tile(b_sp_b[0][:, :DEC_SEQ].T, (DEC_BATCH, 1))
    xs, v_new = _mixb(xs, gn[1, 1], w_uv[0], vgain, w_sp_s, b_t_s, w_ob[0],
                      tm=n_s, chunk=DEC_SEQ, emit_v=True)

    xp = _ffn(xp, gn[1, 2], w_in[1, 1], w_out[1, 1], tm_p)
    xs = _ffn(xs, gn[1, 2], w_in[1, 1], w_out[1, 1], n_s)

    def tail(kv, keep):
        t = kv.reshape(BATCH, SEQ, kv_cols)[:, SEQ - keep:]
        return t.reshape(1, BATCH, keep, 2, HEADS, HEAD_DIM)

    def rows(kv):
        return kv.reshape(1, DEC_BATCH, DEC_SEQ, 2, HEADS, HEAD_DIM)

    return (
        xp.reshape(BATCH, SEQ, D_MODEL),
        xs.reshape(DEC_BATCH, DEC_SEQ, D_MODEL),
        tail(kvp0, min(A_GROUPS[0][0], SEQ)),
        tail(kvp1, min(A_GROUPS[1][0], SEQ)),
        tail(kvp2, min(A_GROUPS[2][0], SEQ)),
        rows(kvs0), rows(kvs1), rows(kvs2),
        v_new.reshape(1, DEC_BATCH, DEC_SEQ, B_WIDTH),
    )
```

```python
import functools

import jax
import jax.numpy as jnp
from jax import lax
from jax.experimental import pallas as pl
from jax.experimental.pallas import tpu as pltpu

F32 = jnp.float32
BF16 = jnp.bfloat16

D_MODEL = 1024
D_FF = 2816
HEAD_DIM = 64
HEADS = 8
GROUP_W = HEADS * HEAD_DIM
QKV_W = 3 * GROUP_W
A_GROUPS = ((128, 1), (512, 4), (2048, 16))
ROPE_THETA = 10000.0
B_WIDTH = 2048
B_GROUPS = 16
B_CHUNK = 128
EPS = 1e-6
NEG = -1e30

SEQ = 2048
BATCH = 16
DEC_BATCH = 32
DEC_SEQ = 8
PAST_LEN = 16384

V7X_VMEM_LIMIT_BYTES = 56 * 1024 * 1024


def _dot(a, b):
    return jnp.dot(a, b, preferred_element_type=F32)


def _dot_nt(a, b):
    return lax.dot_general(a, b, (((1,), (1,)), ((), ())), preferred_element_type=F32)


def _rms(x, g):
    ms = jnp.mean(x * x, axis=-1, keepdims=True)
    return x * lax.rsqrt(ms + EPS) * g


def _gelu_tanh(x):
    cdf = 0.5 * (1.0 + jnp.tanh(0.7978845608028654 * (x + 0.044715 * (x * x * x))))
    return x * cdf


def _params(n_axes):
    return pltpu.CompilerParams(
        dimension_semantics=("arbitrary",) * n_axes,
        vmem_limit_bytes=V7X_VMEM_LIMIT_BYTES,
    )


def _resident(shape, index=None):
    if index is None:
        index = (0,) * len(shape)
    return pl.BlockSpec(shape, lambda *_: index, pipeline_mode=pl.Buffered(1))


FF_CHUNK = 256


def _ffn_kernel(*refs, with_proj):
    if with_proj:
        x_ref, m_ref, wo_ref, g_ref, win_ref, wout_ref, o_ref, h_ref = refs
        x = x_ref[...] + _dot(m_ref[...].astype(BF16), wo_ref[...])
    else:
        x_ref, g_ref, win_ref, wout_ref, o_ref, h_ref = refs
        x = x_ref[...]
    xn = _rms(x, g_ref[...]).astype(BF16)
    for c in range(D_FF // FF_CHUNK):
        lo = c * FF_CHUNK
        gate = _dot(xn, win_ref[:, lo:lo + FF_CHUNK])
        up = _dot(xn, win_ref[:, D_FF + lo:D_FF + lo + FF_CHUNK])
        act = gate * (1.0 / (1.0 + jnp.exp(-gate)))
        h_ref[:, lo:lo + FF_CHUNK] = (act * up).astype(BF16)
    o_ref[...] = x + 0.5 * _dot(h_ref[...], wout_ref[...])


def _ffn(x, g, w_in, w_out, layer, step, tm, proj=None):
    n = x.shape[0]
    x_spec = pl.BlockSpec((tm, D_MODEL), lambda i: (i, 0))
    in_specs = [x_spec]
    args = [x]
    if proj is not None:
        m, w_o = proj
        in_specs += [pl.BlockSpec((tm, GROUP_W), lambda i: (i, 0)), _resident((GROUP_W, D_MODEL))]
        args += [m, w_o]
    in_specs += [
        _resident((1, D_MODEL)),
        _resident((None, None, D_MODEL, 2 * D_FF), (layer, step, 0, 0)),
        _resident((None, None, D_FF, D_MODEL), (layer, step, 0, 0)),
    ]
    args += [g, w_in, w_out]
    return pl.pallas_call(
        functools.partial(_ffn_kernel, with_proj=proj is not None),
        out_shape=jax.ShapeDtypeStruct((n, D_MODEL), F32),
        grid=(n // tm,),
        in_specs=in_specs,
        out_specs=x_spec,
        scratch_shapes=[pltpu.VMEM((tm, D_FF), BF16)],
        compiler_params=_params(1),
        name="ffn",
    )(*args)


QK_BLOCK = 256
QKV_TM = 512


def _qkv_kernel(x_ref, g_ref, w_ref, gain_ref, cos_ref, sin_ref, bd_ref,
                q_ref, kv0_ref, kv1_ref, kv2_ref, *tail_refs, tiles_per_seq):
    tm = x_ref.shape[0]
    xn = _rms(x_ref[...], g_ref[...]).astype(BF16)
    cos = cos_ref[...]
    sin = sin_ref[...]
    bd = bd_ref[...]
    lane = lax.broadcasted_iota(jnp.int32, (tm, QK_BLOCK), 1)
    first_half = (lane & (HEAD_DIM - 1)) < (HEAD_DIM // 2)

    def norm_rot(y, gain):
        ss = y * y
        hi = ss.astype(BF16)
        lo = (ss - hi.astype(F32)).astype(BF16)
        ms = _dot(hi, bd) + _dot(lo, bd)
        yn = y * lax.rsqrt(ms + EPS) * gain
        partner = jnp.where(first_half,
                            pltpu.roll(yn, QK_BLOCK - HEAD_DIM // 2, 1),
                            pltpu.roll(yn, HEAD_DIM // 2, 1))
        return yn * cos + partner * sin

    if tail_refs:
        is_last_tile = (pl.program_id(0) % tiles_per_seq) == tiles_per_seq - 1

    kv_refs = (kv0_ref, kv1_ref, kv2_ref)
    for g, (window, _) in enumerate(A_GROUPS):
        for half in range(GROUP_W // QK_BLOCK):
            c0 = g * GROUP_W + half * QK_BLOCK
            o0 = half * QK_BLOCK
            yq = _dot(xn, w_ref[:, c0:c0 + QK_BLOCK])
            q_ref[:, c0:c0 + QK_BLOCK] = norm_rot(yq, gain_ref[0:1, :])
            k = norm_rot(_dot(xn, w_ref[:, QKV_W + c0:QKV_W + c0 + QK_BLOCK]), gain_ref[1:2, :])
            v = _dot(xn, w_ref[:, 2 * QKV_W + c0:2 * QKV_W + c0 + QK_BLOCK])
            kv_refs[g][:, o0:o0 + QK_BLOCK] = k
            kv_refs[g][:, GROUP_W + o0:GROUP_W + o0 + QK_BLOCK] = v
            if not tail_refs:
                continue
            t_ref = tail_refs[g]
            keep = min(window, tm)

            def put_tail(t_ref=t_ref, k=k, v=v, o0=o0, keep=keep):
                t_ref[0, o0:o0 + QK_BLOCK, :] = k[tm - keep:, :].T
                t_ref[1, o0:o0 + QK_BLOCK, :] = v[tm - keep:, :].T

            if window >= SEQ:
                put_tail()
            else:
                pl.when(is_last_tile)(put_tail)


def _qkv(x, g, w, gain, cos, sin, bd, tm, with_tails):
    n = x.shape[0]
    pos_blocks = cos.shape[0] // tm
    kv_shape = jax.ShapeDtypeStruct((n, 2 * GROUP_W), F32)
    kv_spec = pl.BlockSpec((tm, 2 * GROUP_W), lambda i: (i, 0))
    out_shape = [jax.ShapeDtypeStruct((n, QKV_W), F32), kv_shape, kv_shape, kv_shape]
    out_specs = [pl.BlockSpec((tm, QKV_W), lambda i: (i, 0)), kv_spec, kv_spec, kv_spec]
    if with_tails:
        assert SEQ % tm == 0 and all(min(wd, SEQ) % tm == 0 or wd < tm for wd, _ in A_GROUPS)
        for window, _ in A_GROUPS:
            keep = min(window, SEQ)
            blk = min(keep, tm)
            out_shape.append(jax.ShapeDtypeStruct((n // SEQ, 2, GROUP_W, keep), F32))
            if keep > tm:
                index = lambda i: (i // pos_blocks, 0, 0, i % pos_blocks)
            else:
                index = lambda i: (i // pos_blocks, 0, 0, 0)
            out_specs.append(pl.BlockSpec((None, 2, GROUP_W, blk), index))
    return pl.pallas_call(
        functools.partial(_qkv_kernel, tiles_per_seq=pos_blocks),
        out_shape=tuple(out_shape),
        grid=(n // tm,),
        in_specs=[
            pl.BlockSpec((tm, D_MODEL), lambda i: (i, 0)),
            _resident((1, D_MODEL)),
            _resident((D_MODEL, 3 * QKV_W)),
            _resident((2, QK_BLOCK)),
            pl.BlockSpec((tm, QK_BLOCK), lambda i: (i % pos_blocks, 0)),
            pl.BlockSpec((tm, QK_BLOCK), lambda i: (i % pos_blocks, 0)),
            _resident((QK_BLOCK, QK_BLOCK)),
        ],
        out_specs=tuple(out_specs),
        compiler_params=_params(1),
        name="qkv",
    )(x, g, w, gain, cos, sin, bd)


BLK = 128
PAIR_W = 128
SPAN = 4


def _attn_kernel(q0_ref, q1_ref, q2_ref, k0_ref, v0_ref, k1_ref, v1_ref, k2_ref, v2_ref,
                 o_ref, og0, og1, og2, lg0, lg1, lg2, bias_ref):
    lane = lax.broadcasted_iota(jnp.int32, (BLK, PAIR_W), 1)
    head_a = lane < HEAD_DIM
    row2 = lax.broadcasted_iota(jnp.int32, (2 * BLK, PAIR_W), 0)
    lane2 = lax.broadcasted_iota(jnp.int32, (2 * BLK, PAIR_W), 1)
    q_keep = (row2 >> 7) == (lane2 >> 6)
    iq = lax.broadcasted_iota(jnp.int32, (2 * BLK, 2 * BLK), 0) & (BLK - 1)
    col = lax.broadcasted_iota(jnp.int32, (2 * BLK, 2 * BLK), 1)
    jk = col & (BLK - 1)
    valid = ((col < BLK) & (jk >= iq)) | ((col >= BLK) & (jk <= iq))
    bias_ref[...] = jnp.where(valid, 0.0, NEG)

    def one_block(q, kb, vb, bias):
        q = q * (HEAD_DIM ** -0.5)
        q2 = jnp.where(q_keep, jnp.concatenate([q, q], axis=0), 0.0).astype(BF16)
        s = _dot_nt(q2, kb) + bias
        m = jnp.max(s, axis=1, keepdims=True)
        p = jnp.exp(s - m)
        l = jnp.sum(p, axis=1, keepdims=True)
        o = _dot(p.astype(BF16), vb) / l
        lse = m + jnp.log(l)
        return (jnp.where(head_a, o[:BLK], o[BLK:]), jnp.where(head_a, lse[:BLK], lse[BLK:]))

    def span(refs, dil, r, n0, count, has_prev):
        q_ref, k_ref, v_ref, og_ref, lg_ref = refs

        def rows(n):
            return pl.ds(r + (n0 + n) * (dil * BLK), BLK, stride=dil)

        first = -1 if has_prev else 0
        kb = {n: k_ref[rows(n), :].astype(BF16) for n in range(first, count)}
        vb = {n: v_ref[rows(n), :].astype(BF16) for n in range(first, count)}
        for n in range(count):
            q = q_ref[rows(n), :]
            if n - 1 in kb:
                kcat = jnp.concatenate([kb[n - 1], kb[n]], axis=0)
                vcat = jnp.concatenate([vb[n - 1], vb[n]], axis=0)
                o, lse = one_block(q, kcat, vcat, bias_ref[...])
            else:
                o, lse = one_block(q, kb[n], vb[n], bias_ref[:, BLK:])
            og_ref[rows(n), :] = o
            lg_ref[rows(n), :] = lse

    refs0 = (q0_ref, k0_ref, v0_ref, og0, lg0)
    refs1 = (q1_ref, k1_ref, v1_ref, og1, lg1)
    refs2 = (q2_ref, k2_ref, v2_ref, og2, lg2)
    dil0, dil1, dil2 = (d for _, d in A_GROUPS)
    blocks0, blocks1, blocks2 = (SEQ // (d * BLK) for _, d in A_GROUPS)
    assert blocks0 % SPAN == 0 and blocks1 == SPAN and blocks2 == 1 and dil2 % SPAN == 0

    span(refs0, dil0, 0, 0, SPAN, False)

    def body0(it, carry):
        span(refs0, dil0, 0, it * SPAN, SPAN, True)
        return carry

    lax.fori_loop(1, blocks0 // SPAN, body0, 0)

    def body1(r, carry):
        span(refs1, dil1, r, 0, SPAN, False)
        return carry

    lax.fori_loop(0, dil1, body1, 0)

    def body2(it, carry):
        for u in range(SPAN):
            span(refs2, dil2, it * SPAN + u, 0, 1, False)
        return carry

    lax.fori_loop(0, dil2 // SPAN, body2, 0)

    def merge(i, carry):
        rows = pl.ds(pl.multiple_of(i * BLK, BLK), BLK)
        l0, l1, l2 = lg0[rows, :], lg1[rows, :], lg2[rows, :]
        mx = jnp.maximum(jnp.maximum(l0, l1), l2)
        e0, e1, e2 = jnp.exp(l0 - mx), jnp.exp(l1 - mx), jnp.exp(l2 - mx)
        num = e0 * og0[rows, :] + e1 * og1[rows, :] + e2 * og2[rows, :]
        o_ref[rows, :] = (num / (e0 + e1 + e2)).astype(o_ref.dtype)
        return carry

    lax.fori_loop(0, SEQ // BLK, merge, 0)


def _attn_prompt(q, kv0, kv1, kv2):
    n = q.shape[0]
    pairs = GROUP_W // PAIR_W
    blk = (SEQ, PAIR_W)
    q_specs = [pl.BlockSpec(blk, functools.partial(lambda b, hp, g: (b, g * pairs + hp), g=g))
               for g in range(3)]
    k_spec = pl.BlockSpec(blk, lambda b, hp: (b, hp))
    v_spec = pl.BlockSpec(blk, lambda b, hp: (b, pairs + hp))
    scratch = [pltpu.VMEM(blk, F32) for _ in range(6)] + [pltpu.VMEM((2 * BLK, 2 * BLK), F32)]
    return pl.pallas_call(
        _attn_kernel,
        out_shape=jax.ShapeDtypeStruct((n, GROUP_W), BF16),
        grid=(n // SEQ, pairs),
        in_specs=q_specs + [k_spec, v_spec, k_spec, v_spec, k_spec, v_spec],
        out_specs=pl.BlockSpec(blk, lambda b, hp: (b, hp)),
        scratch_shapes=scratch,
        compiler_params=_params(2),
        name="attn_prompt",
    )(q, q, q, kv0, kv0, kv1, kv1, kv2, kv2)


NEW_PAD = 128
QROWS = HEADS * DEC_SEQ


def _attn_sample_kernel(q_ref, n0_ref, n1_ref, n2_ref, c0_ref, c1_ref, c2_ref, o_ref):
    new_refs = (n0_ref, n1_ref, n2_ref)
    cache_refs = (c0_ref, c1_ref, c2_ref)
    rowq = lax.broadcasted_iota(jnp.int32, (QROWS, GROUP_W), 0)
    laneq = lax.broadcasted_iota(jnp.int32, (QROWS, GROUP_W), 1)
    own_head = (rowq >> 3) == (laneq >> 6)
    outs, lses = [], []
    for g, (window, dil) in enumerate(A_GROUPS):
        q = q_ref[:, g * GROUP_W:(g + 1) * GROUP_W] * (HEAD_DIM ** -0.5)
        qbd = jnp.where(own_head, jnp.concatenate([q] * HEADS, axis=0), 0.0).astype(BF16)
        k_old = cache_refs[g][0].astype(BF16)
        v_old = cache_refs[g][1].astype(BF16)
        new = jnp.concatenate(
            [new_refs[g][...], jnp.zeros((NEW_PAD - DEC_SEQ, 2 * GROUP_W), F32)], axis=0)
        k_new = new[:, :GROUP_W].astype(BF16)
        v_new = new[:, GROUP_W:].astype(BF16)

        key = lax.broadcasted_iota(jnp.int32, (QROWS, window), 1)
        j = lax.broadcasted_iota(jnp.int32, (QROWS, window), 0) & (DEC_SEQ - 1)
        back = window + j - key
        valid_old = ((back & (dil - 1)) == 0) & (key >= j)
        jn = lax.broadcasted_iota(jnp.int32, (QROWS, NEW_PAD), 1)
        jq = lax.broadcasted_iota(jnp.int32, (QROWS, NEW_PAD), 0) & (DEC_SEQ - 1)
        valid_new = (jn <= jq) & (((jq - jn) & (dil - 1)) == 0)

        s_old = jnp.where(valid_old, _dot(qbd, k_old), NEG)
        s_new = jnp.where(valid_new, _dot_nt(qbd, k_new), NEG)
        m = jnp.maximum(jnp.max(s_old, axis=1, keepdims=True),
                        jnp.max(s_new, axis=1, keepdims=True))
        p_old = jnp.exp(s_old - m)
        p_new = jnp.exp(s_new - m)
        l = jnp.sum(p_old, axis=1, keepdims=True) + jnp.sum(p_new, axis=1, keepdims=True)
        o = (_dot_nt(p_old.astype(BF16), v_old) + _dot(p_new.astype(BF16), v_new)) / l
        lse = m + jnp.log(l)
        om = jnp.where(own_head, o, 0.0)
        lm = jnp.where(own_head, lse, 0.0)
        og, lg = om[0:DEC_SEQ], lm[0:DEC_SEQ]
        for h in range(1, HEADS):
            og = og + om[h * DEC_SEQ:(h + 1) * DEC_SEQ]
            lg = lg + lm[h * DEC_SEQ:(h + 1) * DEC_SEQ]
        outs.append(og)
        lses.append(lg)
    mx = jnp.maximum(jnp.maximum(lses[0], lses[1]), lses[2])
    e = [jnp.exp(x - mx) for x in lses]
    num = e[0] * outs[0] + e[1] * outs[1] + e[2] * outs[2]
    o_ref[...] = num / (e[0] + e[1] + e[2])


def _attn_sample(q, new0, new1, new2, c0, c1, c2):
    row_spec = lambda w: pl.BlockSpec((DEC_SEQ, w), lambda b: (b, 0))
    cache_spec = lambda c: pl.BlockSpec((None,) + c.shape[1:], lambda b: (b, 0, 0, 0))
    return pl.pallas_call(
        _attn_sample_kernel,
        out_shape=jax.ShapeDtypeStruct((DEC_BATCH * DEC_SEQ, GROUP_W), F32),
        grid=(DEC_BATCH,),
        in_specs=[
            row_spec(QKV_W), row_spec(2 * GROUP_W), row_spec(2 * GROUP_W), row_spec(2 * GROUP_W),
            cache_spec(c0), cache_spec(c1), cache_spec(c2),
        ],
        out_specs=row_spec(GROUP_W),
        compiler_params=_params(1),
        name="attn_sample",
    )(q, new0, new1, new2, c0, c1, c2)


UV_CHUNK = 512
B_GROUP_DIM = B_WIDTH // B_GROUPS


def _mixb_kernel(x_ref, g_ref, wuv_ref, vgain_ref, wsp_ref, bt_ref, wo_ref, *rest,
                 chunk, emit_v):
    if emit_v:
        o_ref, vout_ref, u_ref, v_ref, h_ref = rest
    else:
        o_ref, u_ref, v_ref, h_ref = rest
    tm = x_ref.shape[0]
    r = wsp_ref.shape[1]
    nsub = tm // r
    x = x_ref[...]
    xn = _rms(x, g_ref[...]).astype(BF16)
    ssq = jnp.zeros((tm, 1), F32)
    for c in range(B_WIDTH // UV_CHUNK):
        lo = c * UV_CHUNK
        u_ref[:, lo:lo + UV_CHUNK] = _gelu_tanh(_dot(xn, wuv_ref[:, lo:lo + UV_CHUNK]))
        vv = _gelu_tanh(_dot(xn, wuv_ref[:, B_WIDTH + lo:B_WIDTH + lo + UV_CHUNK]))
        v_ref[:, lo:lo + UV_CHUNK] = vv
        ssq = ssq + jnp.sum(vv * vv, axis=-1, keepdims=True)
    inv = lax.rsqrt(ssq * (1.0 / B_WIDTH) + EPS)
    row = lax.broadcasted_iota(jnp.int32, (r, r), 0)
    col = lax.broadcasted_iota(jnp.int32, (r, r), 1)
    shift = chunk.bit_length() - 1
    keep = ((row >> shift) == (col >> shift)) & ((col & (chunk - 1)) <= (row & (chunk - 1)))
    for g in range(B_GROUPS):
        lanes = slice(g * B_GROUP_DIM, (g + 1) * B_GROUP_DIM)
        vg = v_ref[:, lanes] * inv * vgain_ref[:, lanes]
        if emit_v:
            vout_ref[:, lanes] = vg
        vb = vg.astype(BF16)
        if nsub > 1:
            vb = jnp.concatenate([vb[s * r:(s + 1) * r] for s in range(nsub)], axis=1)
        w = jnp.where(keep, wsp_ref[g], 0.0).astype(BF16)
        mixed = _dot(w, vb) + bt_ref[:, g:g + 1]
        for s in range(nsub):
            rows = slice(s * r, (s + 1) * r)
            h_ref[rows, lanes] = (
                u_ref[rows, lanes] * mixed[:, s * B_GROUP_DIM:(s + 1) * B_GROUP_DIM]
            ).astype(BF16)
    o_ref[...] = x + _dot(h_ref[...], wo_ref[...])


def _mixb(x, g, w_uv, v_gain, w_sp, b_t, w_o, tm, chunk, emit_v):
    n = x.shape[0]
    r = w_sp.shape[1]
    x_spec = pl.BlockSpec((tm, D_MODEL), lambda i: (i, 0))
    out_shape = [jax.ShapeDtypeStruct((n, D_MODEL), F32)]
    out_specs = [x_spec]
    if emit_v:
        out_shape.append(jax.ShapeDtypeStruct((n, B_WIDTH), F32))
        out_specs.append(pl.BlockSpec((tm, B_WIDTH), lambda i: (i, 0)))
    return pl.pallas_call(
        functools.partial(_mixb_kernel, chunk=chunk, emit_v=emit_v),
        out_shape=tuple(out_shape),
        grid=(n // tm,),
        in_specs=[
            x_spec,
            _resident((1, D_MODEL)),
            _resident((D_MODEL, 2 * B_WIDTH)),
            _resident((1, B_WIDTH)),
            _resident((B_GROUPS, r, r)),
            _resident((r, B_GROUPS)),
            _resident((B_WIDTH, D_MODEL)),
        ],
        out_specs=tuple(out_specs),
        scratch_shapes=[
            pltpu.VMEM((tm, B_WIDTH), F32),
            pltpu.VMEM((tm, B_WIDTH), F32),
            pltpu.VMEM((tm, B_WIDTH), BF16),
        ],
        compiler_params=_params(1),
        name="spatial_gating",
    )(x, g, w_uv, v_gain, w_sp, b_t, w_o)


def _rope_tables(pos):
    half = HEAD_DIM // 2
    inv = ROPE_THETA ** (-jnp.arange(half, dtype=F32) / half)
    ang = pos.astype(F32)[:, None] * inv[None, :]
    cos = jnp.cos(ang)
    sin = jnp.sin(ang)
    reps = QK_BLOCK // HEAD_DIM
    cos_t = jnp.tile(jnp.concatenate([cos, cos], axis=-1), (1, reps))
    sin_t = jnp.tile(jnp.concatenate([-sin, sin], axis=-1), (1, reps))
    return cos_t, sin_t


def _feature_major(cache):
    _, b, length, two, h, hd = cache.shape
    return jnp.transpose(cache[0], (0, 2, 3, 4, 1)).reshape(b, two, h * hd, length)


def _position_major(tail):
    b, two, _, length = tail.shape
    t = tail.reshape(b, two, HEADS, HEAD_DIM, length)
    return jnp.transpose(t, (0, 4, 1, 2, 3))[None]


def kernel(x_prompt, x_sample, cache_kv_w128, cache_kv_w512, cache_kv_w2048, norm_g,
           w_ffn_in, w_ffn_out, w_qkv_a, qk_gain_a, w_o_a, w_uv_b, v_gain_b, w_sp_b, b_sp_b,
           w_o_b):
    xp = x_prompt.reshape(BATCH * SEQ, D_MODEL)
    xs = x_sample.reshape(DEC_BATCH * DEC_SEQ, D_MODEL)
    n_s = DEC_BATCH * DEC_SEQ
    tm_p = 1024

    w_in = w_ffn_in.astype(BF16)
    w_out = w_ffn_out.astype(BF16)
    w_qkv = w_qkv_a[0].astype(BF16)
    w_oa = w_o_a[0].astype(BF16)
    w_uv = w_uv_b[0].astype(BF16)
    w_ob = w_o_b[0].astype(BF16)
    gn = norm_g.reshape(2, 3, 1, D_MODEL)

    head_id = jnp.arange(QK_BLOCK) // HEAD_DIM
    bd = jnp.where(head_id[:, None] == head_id[None, :], 1.0 / HEAD_DIM, 0.0).astype(BF16)
    gain = jnp.tile(qk_gain_a[0], (1, QK_BLOCK // HEAD_DIM))
    cos_p, sin_p = _rope_tables(jnp.arange(SEQ))
    cos_s, sin_s = _rope_tables(PAST_LEN + jnp.arange(DEC_SEQ))
    cos_s = jnp.tile(cos_s, (DEC_BATCH, 1))
    sin_s = jnp.tile(sin_s, (DEC_BATCH, 1))

    xp = _ffn(xp, gn[0, 0], w_in, w_out, 0, 0, tm_p)
    xs = _ffn(xs, gn[0, 0], w_in, w_out, 0, 0, n_s)

    qp, kvp0, kvp1, kvp2, tail0, tail1, tail2 = _qkv(
        xp, gn[0, 1], w_qkv, gain, cos_p, sin_p, bd, QKV_TM, True)
    qs, kvs0, kvs1, kvs2 = _qkv(xs, gn[0, 1], w_qkv, gain, cos_s, sin_s, bd, n_s, False)

    mp = _attn_prompt(qp, kvp0, kvp1, kvp2)
    ms = _attn_sample(qs, kvs0, kvs1, kvs2, _feature_major(cache_kv_w128),
                      _feature_major(cache_kv_w512), _feature_major(cache_kv_w2048))

    xp = _ffn(xp, gn[0, 2], w_in, w_out, 0, 1, tm_p, proj=(mp, w_oa))
    xs = _ffn(xs, gn[0, 2], w_in, w_out, 0, 1, n_s, proj=(ms, w_oa))

    xp = _ffn(xp, gn[1, 0], w_in, w_out, 1, 0, tm_p)
    xs = _ffn(xs, gn[1, 0], w_in, w_out, 1, 0, n_s)

    vgain = v_gain_b[0].reshape(1, B_WIDTH)
    (xp,) = _mixb(xp, gn[1, 1], w_uv, vgain, w_sp_b[0], b_sp_b[0].T, w_ob,
                  tm=512, chunk=B_CHUNK, emit_v=False)
    w_sp_s = jnp.tile(w_sp_b[0][:, :DEC_SEQ, :DEC_SEQ], (1, DEC_BATCH, DEC_BATCH))
    b_t_s = jnp.tile(b_sp_b[0][:, :DEC_SEQ].T, (DEC_BATCH, 1))
    xs, v_new = _mixb(xs, gn[1, 1], w_uv, vgain, w_sp_s, b_t_s, w_ob,
                      tm=n_s, chunk=DEC_SEQ, emit_v=True)

    xp = _ffn(xp, gn[1, 2], w_in, w_out, 1, 1, tm_p)
    xs = _ffn(xs, gn[1, 2], w_in, w_out, 1, 1, n_s)

    def rows(kv):
        return kv.reshape(1, DEC_BATCH, DEC_SEQ, 2, HEADS, HEAD_DIM)

    return (
        xp.reshape(BATCH, SEQ, D_MODEL),
        xs.reshape(DEC_BATCH, DEC_SEQ, D_MODEL),
        _position_major(tail0), _position_major(tail1), _position_major(tail2),
        rows(kvs0), rows(kvs1), rows(kvs2),
        v_new.reshape(1, DEC_BATCH, DEC_SEQ, B_WIDTH),
    )
```

```python
import functools

import jax
import jax.numpy as jnp
from jax import lax
from jax.experimental import pallas as pl
from jax.experimental.pallas import tpu as pltpu

F32 = jnp.float32
BF16 = jnp.bfloat16

D_MODEL = 1024
D_FF = 2816
HEAD_DIM = 64
HEADS = 8
GROUP_W = HEADS * HEAD_DIM
QKV_W = 3 * GROUP_W
A_GROUPS = ((128, 1), (512, 4), (2048, 16))
ROPE_THETA = 10000.0
B_WIDTH = 2048
B_GROUPS = 16
B_CHUNK = 128
EPS = 1e-6
NEG = -1e30

SEQ = 2048
BATCH = 16
DEC_BATCH = 32
DEC_SEQ = 8
PAST_LEN = 16384

V7X_VMEM_LIMIT_BYTES = 56 * 1024 * 1024


def _dot(a, b):
    return jnp.dot(a, b, preferred_element_type=F32)


def _dot_nt(a, b):
    return lax.dot_general(a, b, (((1,), (1,)), ((), ())), preferred_element_type=F32)


def _rms(x, g):
    ms = jnp.mean(x * x, axis=-1, keepdims=True)
    return x * lax.rsqrt(ms + EPS) * g


def _gelu_tanh(x):
    cdf = 0.5 * (1.0 + jnp.tanh(0.7978845608028654 * (x + 0.044715 * (x * x * x))))
    return x * cdf


def _params(n_axes):
    return pltpu.CompilerParams(
        dimension_semantics=("arbitrary",) * n_axes,
        vmem_limit_bytes=V7X_VMEM_LIMIT_BYTES,
    )


def _resident(shape, index=None):
    if index is None:
        index = (0,) * len(shape)
    return pl.BlockSpec(shape, lambda *_: index, pipeline_mode=pl.Buffered(1))


FF_CHUNK = 256


def _ffn_kernel(*refs, with_proj):
    if with_proj:
        x_ref, m_ref, wo_ref, g_ref, win_ref, wout_ref, o_ref, h_ref = refs
        x = x_ref[...] + _dot(m_ref[...].astype(BF16), wo_ref[...])
    else:
        x_ref, g_ref, win_ref, wout_ref, o_ref, h_ref = refs
        x = x_ref[...]
    xn = _rms(x, g_ref[...]).astype(BF16)
    for c in range(D_FF // FF_CHUNK):
        lo = c * FF_CHUNK
        gate = _dot(xn, win_ref[:, lo:lo + FF_CHUNK])
        up = _dot(xn, win_ref[:, D_FF + lo:D_FF + lo + FF_CHUNK])
        act = gate * (1.0 / (1.0 + jnp.exp(-gate)))
        h_ref[:, lo:lo + FF_CHUNK] = (act * up).astype(BF16)
    o_ref[...] = x + 0.5 * _dot(h_ref[...], wout_ref[...])


def _ffn(x, g, w_in, w_out, layer, step, tm, proj=None):
    n = x.shape[0]
    x_spec = pl.BlockSpec((tm, D_MODEL), lambda i: (i, 0))
    in_specs = [x_spec]
    args = [x]
    if proj is not None:
        m, w_o = proj
        in_specs += [pl.BlockSpec((tm, GROUP_W), lambda i: (i, 0)), _resident((GROUP_W, D_MODEL))]
        args += [m, w_o]
    in_specs += [
        _resident((1, D_MODEL)),
        _resident((None, None, D_MODEL, 2 * D_FF), (layer, step, 0, 0)),
        _resident((None, None, D_FF, D_MODEL), (layer, step, 0, 0)),
    ]
    args += [g, w_in, w_out]
    return pl.pallas_call(
        functools.partial(_ffn_kernel, with_proj=proj is not None),
        out_shape=jax.ShapeDtypeStruct((n, D_MODEL), F32),
        grid=(n // tm,),
        in_specs=in_specs,
        out_specs=x_spec,
        scratch_shapes=[pltpu.VMEM((tm, D_FF), BF16)],
        compiler_params=_params(1),
        name="ffn",
    )(*args)


QK_BLOCK = 256
QKV_TM = 512


def _qkv_kernel(x_ref, g_ref, w_ref, gain_ref, cos_ref, sin_ref, bd_ref,
                q_ref, kv0_ref, kv1_ref, kv2_ref):
    tm = x_ref.shape[0]
    xn = _rms(x_ref[...], g_ref[...]).astype(BF16)
    cos = cos_ref[...]
    sin = sin_ref[...]
    bd = bd_ref[...]
    lane = lax.broadcasted_iota(jnp.int32, (tm, QK_BLOCK), 1)
    first_half = (lane & (HEAD_DIM - 1)) < (HEAD_DIM // 2)

    def norm_rot(y, gain):
        ms = _dot((y * y).astype(BF16), bd)
        yn = y * lax.rsqrt(ms + EPS) * gain
        partner = jnp.where(first_half,
                            pltpu.roll(yn, QK_BLOCK - HEAD_DIM // 2, 1),
                            pltpu.roll(yn, HEAD_DIM // 2, 1))
        return yn * cos + partner * sin

    kv_refs = (kv0_ref, kv1_ref, kv2_ref)
    for g in range(len(A_GROUPS)):
        for half in range(GROUP_W // QK_BLOCK):
            c0 = g * GROUP_W + half * QK_BLOCK
            o0 = half * QK_BLOCK
            yq = _dot(xn, w_ref[:, c0:c0 + QK_BLOCK])
            q_ref[:, c0:c0 + QK_BLOCK] = norm_rot(yq, gain_ref[0:1, :])
            yk = _dot(xn, w_ref[:, QKV_W + c0:QKV_W + c0 + QK_BLOCK])
            kv_refs[g][:, o0:o0 + QK_BLOCK] = norm_rot(yk, gain_ref[1:2, :])
            kv_refs[g][:, GROUP_W + o0:GROUP_W + o0 + QK_BLOCK] = _dot(
                xn, w_ref[:, 2 * QKV_W + c0:2 * QKV_W + c0 + QK_BLOCK])


def _qkv(x, g, w, gain, cos, sin, bd, tm):
    n = x.shape[0]
    pos_blocks = cos.shape[0] // tm
    kv_shape = jax.ShapeDtypeStruct((n, 2 * GROUP_W), F32)
    kv_spec = pl.BlockSpec((tm, 2 * GROUP_W), lambda i: (i, 0))
    return pl.pallas_call(
        _qkv_kernel,
        out_shape=(jax.ShapeDtypeStruct((n, QKV_W), F32), kv_shape, kv_shape, kv_shape),
        grid=(n // tm,),
        in_specs=[
            pl.BlockSpec((tm, D_MODEL), lambda i: (i, 0)),
            _resident((1, D_MODEL)),
            _resident((D_MODEL, 3 * QKV_W)),
            _resident((2, QK_BLOCK)),
            pl.BlockSpec((tm, QK_BLOCK), lambda i: (i % pos_blocks, 0)),
            pl.BlockSpec((tm, QK_BLOCK), lambda i: (i % pos_blocks, 0)),
            _resident((QK_BLOCK, QK_BLOCK)),
        ],
        out_specs=(pl.BlockSpec((tm, QKV_W), lambda i: (i, 0)), kv_spec, kv_spec, kv_spec),
        compiler_params=_params(1),
        name="qkv",
    )(x, g, w, gain, cos, sin, bd)


TAIL_TM = 512


def _tails_kernel(x_ref, g_ref, wt_ref, gain_ref, cos_ref, sin_ref, t0_ref, t1_ref, t2_ref,
                  *, tiles_per_seq):
    tm = x_ref.shape[0]
    half = HEAD_DIM // 2
    xn = _rms(x_ref[...], g_ref[...]).astype(BF16)
    gain = gain_ref[...]
    is_last_tile = (pl.program_id(0) % tiles_per_seq) == tiles_per_seq - 1

    def put_tail(g, t_ref):
        keep = t_ref.shape[-1]
        lo = tm - keep
        kv = _dot_nt(wt_ref[g], xn[lo:, :])
        cos = cos_ref[:, lo:]
        sin = sin_ref[:, lo:]
        for h in range(HEADS):
            y = kv[h * HEAD_DIM:(h + 1) * HEAD_DIM, :]
            ms = jnp.mean(y * y, axis=0, keepdims=True)
            yn = y * lax.rsqrt(ms + EPS) * gain
            x1, x2 = yn[:half], yn[half:]
            t_ref[0, h * HEAD_DIM:h * HEAD_DIM + half, :] = x1 * cos - x2 * sin
            t_ref[0, h * HEAD_DIM + half:(h + 1) * HEAD_DIM, :] = x2 * cos + x1 * sin
        t_ref[1] = kv[GROUP_W:]

    for g, t_ref in enumerate((t0_ref, t1_ref, t2_ref)):
        if A_GROUPS[g][0] >= SEQ:
            put_tail(g, t_ref)
        else:
            pl.when(is_last_tile)(functools.partial(put_tail, g, t_ref))


def _tails(x, g, wt, gain_col, cos_t, sin_t):
    n = x.shape[0]
    tm = TAIL_TM
    pos_blocks = SEQ // tm
    out_shape, out_specs = [], []
    for window, _ in A_GROUPS:
        keep = min(window, SEQ)
        assert keep <= tm or keep == SEQ
        out_shape.append(jax.ShapeDtypeStruct((n // SEQ, 2, GROUP_W, keep), F32))
        if keep > tm:
            index = lambda i: (i // pos_blocks, 0, 0, i % pos_blocks)
        else:
            index = lambda i: (i // pos_blocks, 0, 0, 0)
        out_specs.append(pl.BlockSpec((None, 2, GROUP_W, min(keep, tm)), index))
    return pl.pallas_call(
        functools.partial(_tails_kernel, tiles_per_seq=pos_blocks),
        out_shape=tuple(out_shape),
        grid=(n // tm,),
        in_specs=[
            pl.BlockSpec((tm, D_MODEL), lambda i: (i, 0)),
            _resident((1, D_MODEL)),
            _resident((len(A_GROUPS), 2 * GROUP_W, D_MODEL)),
            _resident((HEAD_DIM, 1)),
            pl.BlockSpec((HEAD_DIM // 2, tm), lambda i: (0, i % pos_blocks)),
            pl.BlockSpec((HEAD_DIM // 2, tm), lambda i: (0, i % pos_blocks)),
        ],
        out_specs=tuple(out_specs),
        compiler_params=_params(1),
        name="kv_tails",
    )(x, g, wt, gain_col, cos_t, sin_t)


BLK = 128
PAIR_W = 128
SPAN = 4


def _attn_kernel(q0_ref, q1_ref, q2_ref, k0_ref, v0_ref, k1_ref, v1_ref, k2_ref, v2_ref,
                 o_ref, og0, og1, og2, lg0, lg1, lg2, bias_ref):
    lane = lax.broadcasted_iota(jnp.int32, (BLK, PAIR_W), 1)
    head_a = lane < HEAD_DIM
    row2 = lax.broadcasted_iota(jnp.int32, (2 * BLK, PAIR_W), 0)
    lane2 = lax.broadcasted_iota(jnp.int32, (2 * BLK, PAIR_W), 1)
    q_keep = (row2 >> 7) == (lane2 >> 6)
    iq = lax.broadcasted_iota(jnp.int32, (2 * BLK, 2 * BLK), 0) & (BLK - 1)
    col = lax.broadcasted_iota(jnp.int32, (2 * BLK, 2 * BLK), 1)
    jk = col & (BLK - 1)
    valid = ((col < BLK) & (jk >= iq)) | ((col >= BLK) & (jk <= iq))
    bias_ref[...] = jnp.where(valid, 0.0, NEG)

    def one_block(q, kb, vb, bias):
        q = q * (HEAD_DIM ** -0.5)
        q2 = jnp.where(q_keep, jnp.concatenate([q, q], axis=0), 0.0).astype(BF16)
        s = _dot_nt(q2, kb) + bias
        m = jnp.max(s, axis=1, keepdims=True)
        p = jnp.exp(s - m)
        l = jnp.sum(p, axis=1, keepdims=True)
        o = _dot(p.astype(BF16), vb) / l
        lse = m + jnp.log(l)
        return (jnp.where(head_a, o[:BLK], o[BLK:]), jnp.where(head_a, lse[:BLK], lse[BLK:]))

    def span(refs, dil, r, n0, count, has_prev):
        q_ref, k_ref, v_ref, og_ref, lg_ref = refs

        def rows(n):
            return pl.ds(r + (n0 + n) * (dil * BLK), BLK, stride=dil)

        first = -1 if has_prev else 0
        kb = {n: k_ref[rows(n), :].astype(BF16) for n in range(first, count)}
        vb = {n: v_ref[rows(n), :].astype(BF16) for n in range(first, count)}
        for n in range(count):
            q = q_ref[rows(n), :]
            if n - 1 in kb:
                kcat = jnp.concatenate([kb[n - 1], kb[n]], axis=0)
                vcat = jnp.concatenate([vb[n - 1], vb[n]], axis=0)
                o, lse = one_block(q, kcat, vcat, bias_ref[...])
            else:
                o, lse = one_block(q, kb[n], vb[n], bias_ref[:, BLK:])
            og_ref[rows(n), :] = o
            lg_ref[rows(n), :] = lse

    refs0 = (q0_ref, k0_ref, v0_ref, og0, lg0)
    refs1 = (q1_ref, k1_ref, v1_ref, og1, lg1)
    refs2 = (q2_ref, k2_ref, v2_ref, og2, lg2)
    dil0, dil1, dil2 = (d for _, d in A_GROUPS)
    blocks0, blocks1, blocks2 = (SEQ // (d * BLK) for _, d in A_GROUPS)
    assert blocks0 % SPAN == 0 and blocks1 == SPAN and blocks2 == 1 and dil2 % SPAN == 0

    for it in range(blocks0 // SPAN):
        span(refs0, dil0, 0, it * SPAN, SPAN, it > 0)
    for r in range(dil1):
        span(refs1, dil1, r, 0, SPAN, False)
    for r in range(dil2):
        span(refs2, dil2, r, 0, 1, False)

    def merge(i, carry):
        rows = pl.ds(pl.multiple_of(i * BLK, BLK), BLK)
        l0, l1, l2 = lg0[rows, :], lg1[rows, :], lg2[rows, :]
        mx = jnp.maximum(jnp.maximum(l0, l1), l2)
        e0, e1, e2 = jnp.exp(l0 - mx), jnp.exp(l1 - mx), jnp.exp(l2 - mx)
        num = e0 * og0[rows, :] + e1 * og1[rows, :] + e2 * og2[rows, :]
        o_ref[rows, :] = (num / (e0 + e1 + e2)).astype(o_ref.dtype)
        return carry

    lax.fori_loop(0, SEQ // BLK, merge, 0)


def _attn_prompt(q, kv0, kv1, kv2):
    n = q.shape[0]
    pairs = GROUP_W // PAIR_W
    blk = (SEQ, PAIR_W)
    q_specs = [pl.BlockSpec(blk, functools.partial(lambda b, hp, g: (b, g * pairs + hp), g=g))
               for g in range(3)]
    k_spec = pl.BlockSpec(blk, lambda b, hp: (b, hp))
    v_spec = pl.BlockSpec(blk, lambda b, hp: (b, pairs + hp))
    scratch = [pltpu.VMEM(blk, F32) for _ in range(6)] + [pltpu.VMEM((2 * BLK, 2 * BLK), F32)]
    return pl.pallas_call(
        _attn_kernel,
        out_shape=jax.ShapeDtypeStruct((n, GROUP_W), BF16),
        grid=(n // SEQ, pairs),
        in_specs=q_specs + [k_spec, v_spec, k_spec, v_spec, k_spec, v_spec],
        out_specs=pl.BlockSpec(blk, lambda b, hp: (b, hp)),
        scratch_shapes=scratch,
        compiler_params=_params(2),
        name="attn_prompt",
    )(q, q, q, kv0, kv0, kv1, kv1, kv2, kv2)


NEW_PAD = 128
QROWS = HEADS * DEC_SEQ


def _attn_sample_kernel(q_ref, n0_ref, n1_ref, n2_ref, c0_ref, c1_ref, c2_ref, o_ref):
    new_refs = (n0_ref, n1_ref, n2_ref)
    cache_refs = (c0_ref, c1_ref, c2_ref)
    rowq = lax.broadcasted_iota(jnp.int32, (QROWS, GROUP_W), 0)
    laneq = lax.broadcasted_iota(jnp.int32, (QROWS, GROUP_W), 1)
    own_head = (rowq >> 3) == (laneq >> 6)
    outs, lses = [], []
    for g, (window, dil) in enumerate(A_GROUPS):
        q = q_ref[:, g * GROUP_W:(g + 1) * GROUP_W] * (HEAD_DIM ** -0.5)
        qbd = jnp.where(own_head, jnp.concatenate([q] * HEADS, axis=0), 0.0).astype(BF16)
        k_old = cache_refs[g][0].astype(BF16)
        v_old = cache_refs[g][1].astype(BF16)
        new = jnp.concatenate(
            [new_refs[g][...], jnp.zeros((NEW_PAD - DEC_SEQ, 2 * GROUP_W), F32)], axis=0)
        k_new = new[:, :GROUP_W].astype(BF16)
        v_new = new[:, GROUP_W:].astype(BF16)

        key = lax.broadcasted_iota(jnp.int32, (QROWS, window), 1)
        j = lax.broadcasted_iota(jnp.int32, (QROWS, window), 0) & (DEC_SEQ - 1)
        back = window + j - key
        valid_old = ((back & (dil - 1)) == 0) & (key >= j)
        jn = lax.broadcasted_iota(jnp.int32, (QROWS, NEW_PAD), 1)
        jq = lax.broadcasted_iota(jnp.int32, (QROWS, NEW_PAD), 0) & (DEC_SEQ - 1)
        valid_new = (jn <= jq) & (((jq - jn) & (dil - 1)) == 0)

        s_old = jnp.where(valid_old, _dot(qbd, k_old), NEG)
        s_new = jnp.where(valid_new, _dot_nt(qbd, k_new), NEG)
        m = jnp.maximum(jnp.max(s_old, axis=1, keepdims=True),
                        jnp.max(s_new, axis=1, keepdims=True))
        p_old = jnp.exp(s_old - m)
        p_new = jnp.exp(s_new - m)
        l = jnp.sum(p_old, axis=1, keepdims=True) + jnp.sum(p_new, axis=1, keepdims=True)
        o = (_dot_nt(p_old.astype(BF16), v_old) + _dot(p_new.astype(BF16), v_new)) / l
        lse = m + jnp.log(l)
        om = jnp.where(own_head, o, 0.0)
        lm = jnp.where(own_head, lse, 0.0)
        og, lg = om[0:DEC_SEQ], lm[0:DEC_SEQ]
        for h in range(1, HEADS):
            og = og + om[h * DEC_SEQ:(h + 1) * DEC_SEQ]
            lg = lg + lm[h * DEC_SEQ:(h + 1) * DEC_SEQ]
        outs.append(og)
        lses.append(lg)
    mx = jnp.maximum(jnp.maximum(lses[0], lses[1]), lses[2])
    e = [jnp.exp(x - mx) for x in lses]
    num = e[0] * outs[0] + e[1] * outs[1] + e[2] * outs[2]
    o_ref[...] = num / (e[0] + e[1] + e[2])


def _attn_sample(q, new0, new1, new2, c0, c1, c2):
    row_spec = lambda w: pl.BlockSpec((DEC_SEQ, w), lambda b: (b, 0))
    cache_spec = lambda c: pl.BlockSpec((None,) + c.shape[1:], lambda b: (b, 0, 0, 0))
    return pl.pallas_call(
        _attn_sample_kernel,
        out_shape=jax.ShapeDtypeStruct((DEC_BATCH * DEC_SEQ, GROUP_W), F32),
        grid=(DEC_BATCH,),
        in_specs=[
            row_spec(QKV_W), row_spec(2 * GROUP_W), row_spec(2 * GROUP_W), row_spec(2 * GROUP_W),
            cache_spec(c0), cache_spec(c1), cache_spec(c2),
        ],
        out_specs=row_spec(GROUP_W),
        compiler_params=_params(1),
        name="attn_sample",
    )(q, new0, new1, new2, c0, c1, c2)


UV_CHUNK = 256
B_GROUP_DIM = B_WIDTH // B_GROUPS


def _mixb_kernel(x_ref, g_ref, wuv_ref, vgain_ref, wsp_ref, bt_ref, wo_ref, *rest,
                 chunk, emit_v):
    if emit_v:
        o_ref, vout_ref, v_ref, h_ref = rest
    else:
        o_ref, v_ref, h_ref = rest
    tm = x_ref.shape[0]
    r = wsp_ref.shape[1]
    nsub = tm // r
    x = x_ref[...]
    xn = _rms(x, g_ref[...]).astype(BF16)
    ssq = jnp.zeros((tm, 1), F32)
    for c in range(B_WIDTH // UV_CHUNK):
        lo = c * UV_CHUNK
        vv = _gelu_tanh(_dot(xn, wuv_ref[:, B_WIDTH + lo:B_WIDTH + lo + UV_CHUNK]))
        v_ref[:, lo:lo + UV_CHUNK] = vv
        ssq = ssq + jnp.sum(vv * vv, axis=-1, keepdims=True)
    inv = lax.rsqrt(ssq * (1.0 / B_WIDTH) + EPS)
    row = lax.broadcasted_iota(jnp.int32, (r, r), 0)
    col = lax.broadcasted_iota(jnp.int32, (r, r), 1)
    shift = chunk.bit_length() - 1
    keep = ((row >> shift) == (col >> shift)) & ((col & (chunk - 1)) <= (row & (chunk - 1)))
    groups_per_chunk = UV_CHUNK // B_GROUP_DIM
    for c in range(B_WIDTH // UV_CHUNK):
        u = _gelu_tanh(_dot(xn, wuv_ref[:, c * UV_CHUNK:(c + 1) * UV_CHUNK]))
        for gi in range(groups_per_chunk):
            g = c * groups_per_chunk + gi
            lanes = slice(g * B_GROUP_DIM, (g + 1) * B_GROUP_DIM)
            vg = v_ref[:, lanes] * inv * vgain_ref[:, lanes]
            if emit_v:
                vout_ref[:, lanes] = vg
            vb = vg.astype(BF16)
            if nsub > 1:
                vb = jnp.concatenate([vb[s * r:(s + 1) * r] for s in range(nsub)], axis=1)
            w = jnp.where(keep, wsp_ref[g], 0.0).astype(BF16)
            mixed = _dot(w, vb) + bt_ref[:, g:g + 1]
            ug = u[:, gi * B_GROUP_DIM:(gi + 1) * B_GROUP_DIM]
            for s in range(nsub):
                rows = slice(s * r, (s + 1) * r)
                h_ref[rows, lanes] = (
                    ug[rows] * mixed[:, s * B_GROUP_DIM:(s + 1) * B_GROUP_DIM]
                ).astype(BF16)
    o_ref[...] = x + _dot(h_ref[...], wo_ref[...])


def _mixb(x, g, w_uv, v_gain, w_sp, b_t, w_o, tm, chunk, emit_v):
    n = x.shape[0]
    r = w_sp.shape[1]
    x_spec = pl.BlockSpec((tm, D_MODEL), lambda i: (i, 0))
    out_shape = [jax.ShapeDtypeStruct((n, D_MODEL), F32)]
    out_specs = [x_spec]
    if emit_v:
        out_shape.append(jax.ShapeDtypeStruct((n, B_WIDTH), F32))
        out_specs.append(pl.BlockSpec((tm, B_WIDTH), lambda i: (i, 0)))
    return pl.pallas_call(
        functools.partial(_mixb_kernel, chunk=chunk, emit_v=emit_v),
        out_shape=tuple(out_shape),
        grid=(n // tm,),
        in_specs=[
            x_spec,
            _resident((1, D_MODEL)),
            _resident((D_MODEL, 2 * B_WIDTH)),
            _resident((1, B_WIDTH)),
            _resident((B_GROUPS, r, r)),
            _resident((r, B_GROUPS)),
            _resident((B_WIDTH, D_MODEL)),
        ],
        out_specs=tuple(out_specs),
        scratch_shapes=[
            pltpu.VMEM((tm, B_WIDTH), F32),
            pltpu.VMEM((tm, B_WIDTH), BF16),
        ],
        compiler_params=_params(1),
        name="spatial_gating",
    )(x, g, w_uv, v_gain, w_sp, b_t, w_o)


def _rope_angles(pos):
    half = HEAD_DIM // 2
    inv = ROPE_THETA ** (-jnp.arange(half, dtype=F32) / half)
    ang = pos.astype(F32)[:, None] * inv[None, :]
    return jnp.cos(ang), jnp.sin(ang)


def _rope_tables(pos):
    cos, sin = _rope_angles(pos)
    reps = QK_BLOCK // HEAD_DIM
    cos_t = jnp.tile(jnp.concatenate([cos, cos], axis=-1), (1, reps))
    sin_t = jnp.tile(jnp.concatenate([-sin, sin], axis=-1), (1, reps))
    return cos_t, sin_t


def _feature_major(cache):
    _, b, length, two, h, hd = cache.shape
    return jnp.transpose(cache[0], (0, 2, 3, 4, 1)).reshape(b, two, h * hd, length)


def _position_major(tail):
    b, two, _, length = tail.shape
    t = tail.reshape(b, two, HEADS, HEAD_DIM, length)
    return jnp.transpose(t, (0, 4, 1, 2, 3))[None]


def kernel(x_prompt, x_sample, cache_kv_w128, cache_kv_w512, cache_kv_w2048, norm_g,
           w_ffn_in, w_ffn_out, w_qkv_a, qk_gain_a, w_o_a, w_uv_b, v_gain_b, w_sp_b, b_sp_b,
           w_o_b):
    xp = x_prompt.reshape(BATCH * SEQ, D_MODEL)
    xs = x_sample.reshape(DEC_BATCH * DEC_SEQ, D_MODEL)
    n_s = DEC_BATCH * DEC_SEQ
    tm_p = 1024

    w_in = w_ffn_in.astype(BF16)
    w_out = w_ffn_out.astype(BF16)
    w_qkv = w_qkv_a[0].astype(BF16)
    w_oa = w_o_a[0].astype(BF16)
    w_uv = w_uv_b[0].astype(BF16)
    w_ob = w_o_b[0].astype(BF16)
    gn = norm_g.reshape(2, 3, 1, D_MODEL)

    head_id = jnp.arange(QK_BLOCK) // HEAD_DIM
    bd = jnp.where(head_id[:, None] == head_id[None, :], 1.0 / HEAD_DIM, 0.0).astype(BF16)
    gain = jnp.tile(qk_gain_a[0], (1, QK_BLOCK // HEAD_DIM))
    cos_p, sin_p = _rope_tables(jnp.arange(SEQ))
    cos_s, sin_s = _rope_tables(PAST_LEN + jnp.arange(DEC_SEQ))
    cos_s = jnp.tile(cos_s, (DEC_BATCH, 1))
    sin_s = jnp.tile(sin_s, (DEC_BATCH, 1))

    xp = _ffn(xp, gn[0, 0], w_in, w_out, 0, 0, tm_p)
    xs = _ffn(xs, gn[0, 0], w_in, w_out, 0, 0, n_s)

    qp, kvp0, kvp1, kvp2 = _qkv(xp, gn[0, 1], w_qkv, gain, cos_p, sin_p, bd, QKV_TM)
    qs, kvs0, kvs1, kvs2 = _qkv(xs, gn[0, 1], w_qkv, gain, cos_s, sin_s, bd, n_s)

    groups = len(A_GROUPS)
    w_kv_t = jnp.transpose(
        w_qkv_a[0][:, QKV_W:].reshape(D_MODEL, 2, groups, GROUP_W), (2, 1, 3, 0)
    ).reshape(groups, 2 * GROUP_W, D_MODEL).astype(BF16)
    cos_a, sin_a = _rope_angles(jnp.arange(SEQ))
    tail0, tail1, tail2 = _tails(xp, gn[0, 1], w_kv_t, qk_gain_a[0, 1].reshape(HEAD_DIM, 1),
                                 cos_a.T, sin_a.T)

    mp = _attn_prompt(qp, kvp0, kvp1, kvp2)
    ms = _attn_sample(qs, kvs0, kvs1, kvs2, _feature_major(cache_kv_w128),
                      _feature_major(cache_kv_w512), _feature_major(cache_kv_w2048))

    xp = _ffn(xp, gn[0, 2], w_in, w_out, 0, 1, tm_p, proj=(mp, w_oa))
    xs = _ffn(xs, gn[0, 2], w_in, w_out, 0, 1, n_s, proj=(ms, w_oa))

    xp = _ffn(xp, gn[1, 0], w_in, w_out, 1, 0, tm_p)
    xs = _ffn(xs, gn[1, 0], w_in, w_out, 1, 0, n_s)

    vgain = v_gain_b[0].reshape(1, B_WIDTH)
    (xp,) = _mixb(xp, gn[1, 1], w_uv, vgain, w_sp_b[0], b_sp_b[0].T, w_ob,
                  tm=1024, chunk=B_CHUNK, emit_v=False)
    w_sp_s = jnp.tile(w_sp_b[0][:, :DEC_SEQ, :DEC_SEQ], (1, DEC_BATCH, DEC_BATCH))
    b_t_s = jnp.tile(b_sp_b[0][:, :DEC_SEQ].T, (DEC_BATCH, 1))
    xs, v_new = _mixb(xs, gn[1, 1], w_uv, vgain, w_sp_s, b_t_s, w_ob,
                      tm=n_s, chunk=DEC_SEQ, emit_v=True)

    xp = _ffn(xp, gn[1, 2], w_in, w_out, 1, 1, tm_p)
    xs = _ffn(xs, gn[1, 2], w_in, w_out, 1, 1, n_s)

    def rows(kv):
        return kv.reshape(1, DEC_BATCH, DEC_SEQ, 2, HEADS, HEAD_DIM)

    return (
        xp.reshape(BATCH, SEQ, D_MODEL),
        xs.reshape(DEC_BATCH, DEC_SEQ, D_MODEL),
        _position_major(tail0), _position_major(tail1), _position_major(tail2),
        rows(kvs0), rows(kvs1), rows(kvs2),
        v_new.reshape(1, DEC_BATCH, DEC_SEQ, B_WIDTH),
    )
```

```python
import functools

import jax
import jax.numpy as jnp
from jax import lax
from jax.experimental import pallas as pl
from jax.experimental.pallas import tpu as pltpu

F32 = jnp.float32
BF16 = jnp.bfloat16

D_MODEL = 1024
D_FF = 2816
HEAD_DIM = 64
HEADS = 8
GROUP_W = HEADS * HEAD_DIM
PAIR_W = 2 * HEAD_DIM
QKV_W = 3 * GROUP_W
A_GROUPS = ((128, 1), (512, 4), (2048, 16))
ROPE_THETA = 10000.0
B_WIDTH = 2048
B_GROUPS = 16
B_CHUNK = 128
EPS = 1e-6
NEG = -1e30
LOG2_E = 1.4426950408889634
LN_2 = 0.6931471805599453

SEQ = 2048
BATCH = 16
DEC_BATCH = 32
DEC_SEQ = 8
PAST_LEN = 16384

V7X_VMEM_LIMIT_BYTES = 56 * 1024 * 1024


def _dot(a, b):
    return jnp.dot(a, b, preferred_element_type=F32)


def _dot_nt(a, b):
    return lax.dot_general(a, b, (((1,), (1,)), ((), ())), preferred_element_type=F32)


def _rms(x, g):
    ms = jnp.mean(x * x, axis=-1, keepdims=True)
    return x * lax.rsqrt(ms + EPS) * g


def _gelu_tanh(x):
    cdf = 0.5 * (1.0 + jnp.tanh(0.7978845608028654 * (x + 0.044715 * (x * x * x))))
    return x * cdf


def _params(n_axes):
    return pltpu.CompilerParams(
        dimension_semantics=("arbitrary",) * n_axes,
        vmem_limit_bytes=V7X_VMEM_LIMIT_BYTES,
    )


def _resident(shape, index=None):
    if index is None:
        index = (0,) * len(shape)
    return pl.BlockSpec(shape, lambda *_: index, pipeline_mode=pl.Buffered(1))


FF_CHUNK = 256


def _ffn_kernel(*refs, with_proj):
    if with_proj:
        x_ref, m_ref, wo_ref, g_ref, win_ref, wout_ref, o_ref, h_ref = refs
        x = x_ref[...] + _dot(m_ref[...].astype(BF16), wo_ref[...])
    else:
        x_ref, g_ref, win_ref, wout_ref, o_ref, h_ref = refs
        x = x_ref[...]
    xn = _rms(x, g_ref[...]).astype(BF16)
    for c in range(D_FF // FF_CHUNK):
        lo = c * FF_CHUNK
        gate = _dot(xn, win_ref[:, lo:lo + FF_CHUNK])
        up = _dot(xn, win_ref[:, D_FF + lo:D_FF + lo + FF_CHUNK])
        act = gate * (1.0 / (1.0 + jnp.exp(-gate)))
        h_ref[:, lo:lo + FF_CHUNK] = (act * up).astype(BF16)
    o_ref[...] = x + 0.5 * _dot(h_ref[...], wout_ref[...])


def _ffn(x, g, w_in, w_out, layer, step, tm, proj=None):
    n = x.shape[0]
    x_spec = pl.BlockSpec((tm, D_MODEL), lambda i: (i, 0))
    in_specs = [x_spec]
    args = [x]
    if proj is not None:
        m, w_o = proj
        in_specs += [pl.BlockSpec((tm, GROUP_W), lambda i: (i, 0)), _resident((GROUP_W, D_MODEL))]
        args += [m, w_o]
    in_specs += [
        _resident((1, D_MODEL)),
        _resident((None, None, D_MODEL, 2 * D_FF), (layer, step, 0, 0)),
        _resident((None, None, D_FF, D_MODEL), (layer, step, 0, 0)),
    ]
    args += [g, w_in, w_out]
    return pl.pallas_call(
        functools.partial(_ffn_kernel, with_proj=proj is not None),
        out_shape=jax.ShapeDtypeStruct((n, D_MODEL), F32),
        grid=(n // tm,),
        in_specs=in_specs,
        out_specs=x_spec,
        scratch_shapes=[pltpu.VMEM((tm, D_FF), BF16)],
        compiler_params=_params(1),
        name="ffn",
    )(*args)


QK_BLOCK = 256
QKV_TM = 1024


def _qkv_kernel(x_ref, g_ref, w_ref, gain_ref, cos_ref, sin_ref, bd_ref, *out_refs, parts):
    xn = _rms(x_ref[...], g_ref[...]).astype(BF16)
    cos = cos_ref[...]
    sin = sin_ref[...]
    bd = bd_ref[...]

    def norm_rot(y, gain):
        ms = _dot((y * y).astype(BF16), bd)
        yn = y * lax.rsqrt(ms + EPS) * gain
        partner = jnp.concatenate(
            [pltpu.roll(yn[:, lo:lo + PAIR_W], PAIR_W // 2, 1)
             for lo in range(0, QK_BLOCK, PAIR_W)], axis=1)
        return yn * cos + partner * sin

    for part, o_ref in zip(parts, out_refs):
        if part == "q":
            for c0 in range(0, QKV_W, GROUP_W):
                y = _dot(xn, w_ref[:, c0:c0 + GROUP_W])
                for o0 in range(0, GROUP_W, QK_BLOCK):
                    o_ref[:, c0 + o0:c0 + o0 + QK_BLOCK] = norm_rot(
                        y[:, o0:o0 + QK_BLOCK], gain_ref[0:1, :])
            continue
        c0 = part * GROUP_W
        yk = _dot(xn, w_ref[:, QKV_W + c0:QKV_W + c0 + GROUP_W])
        for o0 in range(0, GROUP_W, QK_BLOCK):
            o_ref[:, o0:o0 + QK_BLOCK] = norm_rot(yk[:, o0:o0 + QK_BLOCK], gain_ref[1:2, :])
        o_ref[:, GROUP_W:] = _dot(xn, w_ref[:, 2 * QKV_W + c0:2 * QKV_W + c0 + GROUP_W])


def _qkv(x, g, w, gain, cos, sin, bd, tm, parts):
    n = x.shape[0]
    pos_blocks = cos.shape[0] // tm
    widths = [QKV_W if part == "q" else 2 * GROUP_W for part in parts]
    return pl.pallas_call(
        functools.partial(_qkv_kernel, parts=parts),
        out_shape=tuple(jax.ShapeDtypeStruct((n, wd), F32) for wd in widths),
        grid=(n // tm,),
        in_specs=[
            pl.BlockSpec((tm, D_MODEL), lambda i: (i, 0)),
            _resident((1, D_MODEL)),
            _resident((D_MODEL, 3 * QKV_W)),
            _resident((2, QK_BLOCK)),
            pl.BlockSpec((tm, QK_BLOCK), lambda i: (i % pos_blocks, 0)),
            pl.BlockSpec((tm, QK_BLOCK), lambda i: (i % pos_blocks, 0)),
            _resident((QK_BLOCK, QK_BLOCK)),
        ],
        out_specs=tuple(pl.BlockSpec((tm, wd), lambda i: (i, 0)) for wd in widths),
        compiler_params=_params(1),
        name="qkv",
    )(x, g, w, gain, cos, sin, bd)


TAIL_TM = 512


def _tails_kernel(x_ref, g_ref, wt_ref, gain_ref, cos_ref, sin_ref, t0_ref, t1_ref, t2_ref,
                  *, tiles_per_seq):
    tm = x_ref.shape[0]
    half = HEAD_DIM // 2
    xn = _rms(x_ref[...], g_ref[...]).astype(BF16)
    gain = gain_ref[...]
    is_last_tile = (pl.program_id(0) % tiles_per_seq) == tiles_per_seq - 1

    def put_tail(g, t_ref):
        keep = t_ref.shape[-1]
        lo = tm - keep
        kv = _dot_nt(wt_ref[g], xn[lo:, :])
        cos = cos_ref[:, lo:]
        sin = sin_ref[:, lo:]
        for h in range(HEADS):
            y = kv[h * HEAD_DIM:(h + 1) * HEAD_DIM, :]
            ms = jnp.mean(y * y, axis=0, keepdims=True)
            yn = y * lax.rsqrt(ms + EPS) * gain
            x1, x2 = yn[:half], yn[half:]
            t_ref[0, h * HEAD_DIM:h * HEAD_DIM + half, :] = x1 * cos - x2 * sin
            t_ref[0, h * HEAD_DIM + half:(h + 1) * HEAD_DIM, :] = x2 * cos + x1 * sin
        t_ref[1] = kv[GROUP_W:]

    for g, t_ref in enumerate((t0_ref, t1_ref, t2_ref)):
        if A_GROUPS[g][0] >= SEQ:
            put_tail(g, t_ref)
        else:
            pl.when(is_last_tile)(functools.partial(put_tail, g, t_ref))


def _tails(x, g, wt, gain_col, cos_t, sin_t):
    n = x.shape[0]
    tm = TAIL_TM
    pos_blocks = SEQ // tm
    out_shape, out_specs = [], []
    for window, _ in A_GROUPS:
        keep = min(window, SEQ)
        assert keep <= tm or keep == SEQ
        out_shape.append(jax.ShapeDtypeStruct((n // SEQ, 2, GROUP_W, keep), F32))
        if keep > tm:
            index = lambda i: (i // pos_blocks, 0, 0, i % pos_blocks)
        else:
            index = lambda i: (i // pos_blocks, 0, 0, 0)
        out_specs.append(pl.BlockSpec((None, 2, GROUP_W, min(keep, tm)), index))
    return pl.pallas_call(
        functools.partial(_tails_kernel, tiles_per_seq=pos_blocks),
        out_shape=tuple(out_shape),
        grid=(n // tm,),
        in_specs=[
            pl.BlockSpec((tm, D_MODEL), lambda i: (i, 0)),
            _resident((1, D_MODEL)),
            _resident((len(A_GROUPS), 2 * GROUP_W, D_MODEL)),
            _resident((HEAD_DIM, 1)),
            pl.BlockSpec((HEAD_DIM // 2, tm), lambda i: (0, i % pos_blocks)),
            pl.BlockSpec((HEAD_DIM // 2, tm), lambda i: (0, i % pos_blocks)),
        ],
        out_specs=tuple(out_specs),
        compiler_params=_params(1),
        name="kv_tails",
    )(x, g, wt, gain_col, cos_t, sin_t)


BLK = 128
SPAN = 4


def _attn_kernel(q0_ref, q1_ref, q2_ref, k0_ref, v0_ref, k1_ref, v1_ref, k2_ref, v2_ref,
                 o_ref, og0, og1, og2, lg0, lg1, lg2, bias_ref):
    lane = lax.broadcasted_iota(jnp.int32, (BLK, PAIR_W), 1)
    head_a = lane < HEAD_DIM
    row2 = lax.broadcasted_iota(jnp.int32, (2 * BLK, PAIR_W), 0)
    lane2 = lax.broadcasted_iota(jnp.int32, (2 * BLK, PAIR_W), 1)
    q_keep = (row2 >> 7) == ((lane2 >> 5) & 1)
    iq = lax.broadcasted_iota(jnp.int32, (2 * BLK, 2 * BLK), 0) & (BLK - 1)
    col = lax.broadcasted_iota(jnp.int32, (2 * BLK, 2 * BLK), 1)
    jk = col & (BLK - 1)
    valid = ((col < BLK) & (jk >= iq)) | ((col >= BLK) & (jk <= iq))
    bias_ref[...] = jnp.where(valid, 0.0, NEG)

    def one_block(q, kb, vb, bias):
        q = q * (HEAD_DIM ** -0.5 * LOG2_E)
        q2 = jnp.where(q_keep, jnp.concatenate([q, q], axis=0), 0.0).astype(BF16)
        s = _dot_nt(q2, kb) + bias
        m = jnp.max(s, axis=1, keepdims=True)
        p = jnp.exp2(s - m).astype(BF16)
        ones = jnp.ones((vb.shape[0], PAIR_W), BF16)
        pv = _dot(p, jnp.concatenate([vb, ones], axis=1))
        l = pv[:, PAIR_W:]
        o = pv[:, :PAIR_W] / l
        lse = m * LN_2 + jnp.log(l)
        return (jnp.where(head_a, o[:BLK], o[BLK:]), jnp.where(head_a, lse[:BLK], lse[BLK:]))

    def span(refs, dil, r, n0, count, has_prev):
        q_ref, k_ref, v_ref, og_ref, lg_ref = refs

        def rows(n):
            return pl.ds(r + (n0 + n) * (dil * BLK), BLK, stride=dil)

        first = -1 if has_prev else 0
        kb = {n: k_ref[rows(n), :].astype(BF16) for n in range(first, count)}
        vb = {n: v_ref[rows(n), :].astype(BF16) for n in range(first, count)}
        for n in range(count):
            q = q_ref[rows(n), :]
            if n - 1 in kb:
                kcat = jnp.concatenate([kb[n - 1], kb[n]], axis=0)
                vcat = jnp.concatenate([vb[n - 1], vb[n]], axis=0)
                o, lse = one_block(q, kcat, vcat, bias_ref[...])
            else:
                o, lse = one_block(q, kb[n], vb[n], bias_ref[:, BLK:])
            og_ref[rows(n), :] = o
            lg_ref[rows(n), :] = lse

    refs0 = (q0_ref, k0_ref, v0_ref, og0, lg0)
    refs1 = (q1_ref, k1_ref, v1_ref, og1, lg1)
    refs2 = (q2_ref, k2_ref, v2_ref, og2, lg2)
    dil0, dil1, dil2 = (d for _, d in A_GROUPS)
    blocks0, blocks1, blocks2 = (SEQ // (d * BLK) for _, d in A_GROUPS)
    assert blocks0 % SPAN == 0 and blocks1 == SPAN and blocks2 == 1 and dil2 % SPAN == 0

    for it in range(blocks0 // SPAN):
        span(refs0, dil0, 0, it * SPAN, SPAN, it > 0)
    for r in range(dil1):
        span(refs1, dil1, r, 0, SPAN, False)
    for r in range(dil2):
        span(refs2, dil2, r, 0, 1, False)

    def merge(i, carry):
        rows = pl.ds(pl.multiple_of(i * BLK, BLK), BLK)
        l0, l1, l2 = lg0[rows, :], lg1[rows, :], lg2[rows, :]
        mx = jnp.maximum(jnp.maximum(l0, l1), l2)
        e0, e1, e2 = jnp.exp(l0 - mx), jnp.exp(l1 - mx), jnp.exp(l2 - mx)
        num = e0 * og0[rows, :] + e1 * og1[rows, :] + e2 * og2[rows, :]
        o_ref[rows, :] = (num / (e0 + e1 + e2)).astype(o_ref.dtype)
        return carry

    lax.fori_loop(0, SEQ // BLK, merge, 0)


def _attn_prompt(q, kv0, kv1, kv2):
    n = q.shape[0]
    pairs = GROUP_W // PAIR_W
    blk = (SEQ, PAIR_W)
    q_specs = [pl.BlockSpec(blk, functools.partial(lambda b, hp, g: (b, g * pairs + hp), g=g))
               for g in range(3)]
    k_spec = pl.BlockSpec(blk, lambda b, hp: (b, hp))
    v_spec = pl.BlockSpec(blk, lambda b, hp: (b, pairs + hp))
    scratch = [pltpu.VMEM(blk, F32) for _ in range(6)] + [pltpu.VMEM((2 * BLK, 2 * BLK), F32)]
    return pl.pallas_call(
        _attn_kernel,
        out_shape=jax.ShapeDtypeStruct((n, GROUP_W), BF16),
        grid=(n // SEQ, pairs),
        in_specs=q_specs + [k_spec, v_spec, k_spec, v_spec, k_spec, v_spec],
        out_specs=pl.BlockSpec(blk, lambda b, hp: (b, hp)),
        scratch_shapes=scratch,
        compiler_params=_params(2),
        name="attn_prompt",
    )(q, q, q, kv0, kv0, kv1, kv1, kv2, kv2)


NEW_PAD = 128
QROWS = HEADS * DEC_SEQ


def _attn_sample_kernel(q_ref, n0_ref, n1_ref, n2_ref, c0_ref, c1_ref, c2_ref, o_ref):
    new_refs = (n0_ref, n1_ref, n2_ref)
    cache_refs = (c0_ref, c1_ref, c2_ref)
    rowq = lax.broadcasted_iota(jnp.int32, (QROWS, GROUP_W), 0)
    laneq = lax.broadcasted_iota(jnp.int32, (QROWS, GROUP_W), 1)
    own_head = (rowq >> 3) == (laneq >> 6)
    outs, lses = [], []
    for g, (window, dil) in enumerate(A_GROUPS):
        q = q_ref[:, g * GROUP_W:(g + 1) * GROUP_W] * (HEAD_DIM ** -0.5)
        qbd = jnp.where(own_head, jnp.concatenate([q] * HEADS, axis=0), 0.0).astype(BF16)
        k_old = cache_refs[g][0].astype(BF16)
        v_old = cache_refs[g][1].astype(BF16)
        new = jnp.concatenate(
            [new_refs[g][...], jnp.zeros((NEW_PAD - DEC_SEQ, 2 * GROUP_W), F32)], axis=0)
        k_new = new[:, :GROUP_W].astype(BF16)
        v_new = new[:, GROUP_W:].astype(BF16)

        key = lax.broadcasted_iota(jnp.int32, (QROWS, window), 1)
        j = lax.broadcasted_iota(jnp.int32, (QROWS, window), 0) & (DEC_SEQ - 1)
        back = window + j - key
        valid_old = ((back & (dil - 1)) == 0) & (key >= j)
        jn = lax.broadcasted_iota(jnp.int32, (QROWS, NEW_PAD), 1)
        jq = lax.broadcasted_iota(jnp.int32, (QROWS, NEW_PAD), 0) & (DEC_SEQ - 1)
        valid_new = (jn <= jq) & (((jq - jn) & (dil - 1)) == 0)

        s_old = jnp.where(valid_old, _dot(qbd, k_old), NEG)
        s_new = jnp.where(valid_new, _dot_nt(qbd, k_new), NEG)
        m = jnp.maximum(jnp.max(s_old, axis=1, keepdims=True),
                        jnp.max(s_new, axis=1, keepdims=True))
        p_old = jnp.exp(s_old - m)
        p_new = jnp.exp(s_new - m)
        l = jnp.sum(p_old, axis=1, keepdims=True) + jnp.sum(p_new, axis=1, keepdims=True)
        o = (_dot_nt(p_old.astype(BF16), v_old) + _dot(p_new.astype(BF16), v_new)) / l
        lse = m + jnp.log(l)
        om = jnp.where(own_head, o, 0.0)
        lm = jnp.where(own_head, lse, 0.0)
        og, lg = om[0:DEC_SEQ], lm[0:DEC_SEQ]
        for h in range(1, HEADS):
            og = og + om[h * DEC_SEQ:(h + 1) * DEC_SEQ]
            lg = lg + lm[h * DEC_SEQ:(h + 1) * DEC_SEQ]
        outs.append(og)
        lses.append(lg)
    mx = jnp.maximum(jnp.maximum(lses[0], lses[1]), lses[2])
    e = [jnp.exp(x - mx) for x in lses]
    num = e[0] * outs[0] + e[1] * outs[1] + e[2] * outs[2]
    o_ref[...] = num / (e[0] + e[1] + e[2])


def _attn_sample(q, new0, new1, new2, c0, c1, c2):
    row_spec = lambda w: pl.BlockSpec((DEC_SEQ, w), lambda b: (b, 0))
    cache_spec = lambda c: pl.BlockSpec((None,) + c.shape[1:], lambda b: (b, 0, 0, 0))
    return pl.pallas_call(
        _attn_sample_kernel,
        out_shape=jax.ShapeDtypeStruct((DEC_BATCH * DEC_SEQ, GROUP_W), F32),
        grid=(DEC_BATCH,),
        in_specs=[
            row_spec(QKV_W), row_spec(2 * GROUP_W), row_spec(2 * GROUP_W), row_spec(2 * GROUP_W),
            cache_spec(c0), cache_spec(c1), cache_spec(c2),
        ],
        out_specs=row_spec(GROUP_W),
        compiler_params=_params(1),
        name="attn_sample",
    )(q, new0, new1, new2, c0, c1, c2)


UV_CHUNK = 512
B_GROUP_DIM = B_WIDTH // B_GROUPS


def _mixb_kernel(x_ref, g_ref, wuv_ref, vgain_ref, wsp_ref, bt_ref, wo_ref, *rest,
                 chunk, emit_v):
    if emit_v:
        o_ref, vout_ref, v_ref, h_ref = rest
    else:
        o_ref, v_ref, h_ref = rest
    tm = x_ref.shape[0]
    r = wsp_ref.shape[1]
    nsub = tm // r
    x = x_ref[...]
    xn = _rms(x, g_ref[...]).astype(BF16)
    ssq = jnp.zeros((tm, 1), F32)
    for c in range(B_WIDTH // UV_CHUNK):
        lo = c * UV_CHUNK
        v_raw = _dot(xn, wuv_ref[:, B_WIDTH + lo:B_WIDTH + lo + UV_CHUNK])
        for o in range(0, UV_CHUNK, B_GROUP_DIM):
            vv = _gelu_tanh(v_raw[:, o:o + B_GROUP_DIM])
            v_ref[:, lo + o:lo + o + B_GROUP_DIM] = vv
            ssq = ssq + jnp.sum(vv * vv, axis=-1, keepdims=True)
    inv = lax.rsqrt(ssq * (1.0 / B_WIDTH) + EPS)
    row = lax.broadcasted_iota(jnp.int32, (r, r), 0)
    col = lax.broadcasted_iota(jnp.int32, (r, r), 1)
    shift = chunk.bit_length() - 1
    keep = ((row >> shift) == (col >> shift)) & ((col & (chunk - 1)) <= (row & (chunk - 1)))
    groups_per_chunk = UV_CHUNK // B_GROUP_DIM
    for c in range(B_WIDTH // UV_CHUNK):
        u_raw = _dot(xn, wuv_ref[:, c * UV_CHUNK:(c + 1) * UV_CHUNK])
        for gi in range(groups_per_chunk):
            g = c * groups_per_chunk + gi
            lanes = slice(g * B_GROUP_DIM, (g + 1) * B_GROUP_DIM)
            vg = v_ref[:, lanes] * inv * vgain_ref[:, lanes]
            if emit_v:
                vout_ref[:, lanes] = vg
            vb = vg.astype(BF16)
            if nsub > 1:
                vb = jnp.concatenate([vb[s * r:(s + 1) * r] for s in range(nsub)], axis=1)
            w = jnp.where(keep, wsp_ref[g], 0.0).astype(BF16)
            mixed = _dot(w, vb) + bt_ref[:, g:g + 1]
            ug = _gelu_tanh(u_raw[:, gi * B_GROUP_DIM:(gi + 1) * B_GROUP_DIM])
            for s in range(nsub):
                rows = slice(s * r, (s + 1) * r)
                h_ref[rows, lanes] = (
                    ug[rows] * mixed[:, s * B_GROUP_DIM:(s + 1) * B_GROUP_DIM]
                ).astype(BF16)
    o_ref[...] = x + _dot(h_ref[...], wo_ref[...])


def _mixb(x, g, w_uv, v_gain, w_sp, b_t, w_o, tm, chunk, emit_v):
    n = x.shape[0]
    r = w_sp.shape[1]
    x_spec = pl.BlockSpec((tm, D_MODEL), lambda i: (i, 0))
    out_shape = [jax.ShapeDtypeStruct((n, D_MODEL), F32)]
    out_specs = [x_spec]
    if emit_v:
        out_shape.append(jax.ShapeDtypeStruct((n, B_WIDTH), F32))
        out_specs.append(pl.BlockSpec((tm, B_WIDTH), lambda i: (i, 0)))
    return pl.pallas_call(
        functools.partial(_mixb_kernel, chunk=chunk, emit_v=emit_v),
        out_shape=tuple(out_shape),
        grid=(n // tm,),
        in_specs=[
            x_spec,
            _resident((1, D_MODEL)),
            _resident((D_MODEL, 2 * B_WIDTH)),
            _resident((1, B_WIDTH)),
            _resident((B_GROUPS, r, r)),
            _resident((r, B_GROUPS)),
            _resident((B_WIDTH, D_MODEL)),
        ],
        out_specs=tuple(out_specs),
        scratch_shapes=[
            pltpu.VMEM((tm, B_WIDTH), F32),
            pltpu.VMEM((tm, B_WIDTH), BF16),
        ],
        compiler_params=_params(1),
        name="spatial_gating",
    )(x, g, w_uv, v_gain, w_sp, b_t, w_o)


def _rope_angles(pos):
    half = HEAD_DIM // 2
    inv = ROPE_THETA ** (-jnp.arange(half, dtype=F32) / half)
    ang = pos.astype(F32)[:, None] * inv[None, :]
    return jnp.cos(ang), jnp.sin(ang)


def _pair_order(width):
    half = HEAD_DIM // 2
    n = jnp.arange(width)
    lane = n % PAIR_W
    head, hi = (lane // half) % 2, lane // HEAD_DIM
    return (n // PAIR_W) * PAIR_W + head * HEAD_DIM + hi * half + lane % half


def _rope_tables(pos):
    cos, sin = _rope_angles(pos)
    cos_t = jnp.tile(cos, (1, QK_BLOCK // (HEAD_DIM // 2)))
    sin_t = jnp.tile(jnp.concatenate([-sin, -sin, sin, sin], axis=-1), (1, QK_BLOCK // PAIR_W))
    return cos_t, sin_t


def _feature_major(cache):
    _, b, length, two, h, hd = cache.shape
    return jnp.transpose(cache[0], (0, 2, 3, 4, 1)).reshape(b, two, h * hd, length)


def _position_major(tail):
    b, two, _, length = tail.shape
    t = tail.reshape(b, two, HEADS, HEAD_DIM, length)
    return jnp.transpose(t, (0, 4, 1, 2, 3))[None]


def kernel(x_prompt, x_sample, cache_kv_w128, cache_kv_w512, cache_kv_w2048, norm_g,
           w_ffn_in, w_ffn_out, w_qkv_a, qk_gain_a, w_o_a, w_uv_b, v_gain_b, w_sp_b, b_sp_b,
           w_o_b):
    xp = x_prompt.reshape(BATCH * SEQ, D_MODEL)
    xs = x_sample.reshape(DEC_BATCH * DEC_SEQ, D_MODEL)
    n_s = DEC_BATCH * DEC_SEQ
    tm_p = 1024

    w_in = w_ffn_in.astype(BF16)
    w_out = w_ffn_out.astype(BF16)
    order = _pair_order(QKV_W)
    w_qkv = jnp.concatenate(
        [w_qkv_a[0][:, :QKV_W][:, order], w_qkv_a[0][:, QKV_W:2 * QKV_W][:, order],
         w_qkv_a[0][:, 2 * QKV_W:]], axis=1).astype(BF16)
    w_oa = w_o_a[0].astype(BF16)
    w_uv = w_uv_b[0].astype(BF16)
    w_ob = w_o_b[0].astype(BF16)
    gn = norm_g.reshape(2, 3, 1, D_MODEL)

    natural = order[:QK_BLOCK]
    head_id = natural // HEAD_DIM
    bd = jnp.where(head_id[:, None] == head_id[None, :], 1.0 / HEAD_DIM, 0.0).astype(BF16)
    gain = qk_gain_a[0][:, natural % HEAD_DIM]
    cos_p, sin_p = _rope_tables(jnp.arange(SEQ))
    cos_s, sin_s = _rope_tables(PAST_LEN + jnp.arange(DEC_SEQ))
    cos_s = jnp.tile(cos_s, (DEC_BATCH, 1))
    sin_s = jnp.tile(sin_s, (DEC_BATCH, 1))

    xp = _ffn(xp, gn[0, 0], w_in, w_out, 0, 0, tm_p)
    xs = _ffn(xs, gn[0, 0], w_in, w_out, 0, 0, n_s)

    qp, kvp0 = _qkv(xp, gn[0, 1], w_qkv, gain, cos_p, sin_p, bd, QKV_TM, ("q", 0))
    kvp1, kvp2 = _qkv(xp, gn[0, 1], w_qkv, gain, cos_p, sin_p, bd, QKV_TM, (1, 2))
    qs, kvs0, kvs1, kvs2 = _qkv(xs, gn[0, 1], w_qkv, gain, cos_s, sin_s, bd, n_s, ("q", 0, 1, 2))
    qs = qs[:, order]
    kvs0, kvs1, kvs2 = (
        jnp.concatenate([kv[:, :GROUP_W][:, order[:GROUP_W]], kv[:, GROUP_W:]], axis=1)
        for kv in (kvs0, kvs1, kvs2))

    groups = len(A_GROUPS)
    w_kv_t = jnp.transpose(
        w_qkv_a[0][:, QKV_W:].reshape(D_MODEL, 2, groups, GROUP_W), (2, 1, 3, 0)
    ).reshape(groups, 2 * GROUP_W, D_MODEL).astype(BF16)
    cos_a, sin_a = _rope_angles(jnp.arange(SEQ))
    tail0, tail1, tail2 = _tails(xp, gn[0, 1], w_kv_t, qk_gain_a[0, 1].reshape(HEAD_DIM, 1),
                                 cos_a.T, sin_a.T)

    mp = _attn_prompt(qp, kvp0, kvp1, kvp2)
    ms = _attn_sample(qs, kvs0, kvs1, kvs2, _feature_major(cache_kv_w128),
                      _feature_major(cache_kv_w512), _feature_major(cache_kv_w2048))

    xp = _ffn(xp, gn[0, 2], w_in, w_out, 0, 1, tm_p, proj=(mp, w_oa))
    xs = _ffn(xs, gn[0, 2], w_in, w_out, 0, 1, n_s, proj=(ms, w_oa))

    xp = _ffn(xp, gn[1, 0], w_in, w_out, 1, 0, tm_p)
    xs = _ffn(xs, gn[1, 0], w_in, w_out, 1, 0, n_s)

    vgain = v_gain_b[0].reshape(1, B_WIDTH)
    (xp,) = _mixb(xp, gn[1, 1], w_uv, vgain, w_sp_b[0], b_sp_b[0].T, w_ob,
                  tm=1024, chunk=B_CHUNK, emit_v=False)
    w_sp_s = jnp.tile(w_sp_b[0][:, :DEC_SEQ, :DEC_SEQ], (1, DEC_BATCH, DEC_BATCH))
    b_t_s = jnp.tile(b_sp_b[0][:, :DEC_SEQ].T, (DEC_BATCH, 1))
    xs, v_new = _mixb(xs, gn[1, 1], w_uv, vgain, w_sp_s, b_t_s, w_ob,
                      tm=n_s, chunk=DEC_SEQ, emit_v=True)

    xp = _ffn(xp, gn[1, 2], w_in, w_out, 1, 1, tm_p)
    xs = _ffn(xs, gn[1, 2], w_in, w_out, 1, 1, n_s)

    def rows(kv):
        return kv.reshape(1, DEC_BATCH, DEC_SEQ, 2, HEADS, HEAD_DIM)

    return (
        xp.reshape(BATCH, SEQ, D_MODEL),
        xs.reshape(DEC_BATCH, DEC_SEQ, D_MODEL),
        _position_major(tail0), _position_major(tail1), _position_major(tail2),
        rows(kvs0), rows(kvs1), rows(kvs2),
        v_new.reshape(1, DEC_BATCH, DEC_SEQ, B_WIDTH),
    )
```

```python
import functools

import jax
import jax.numpy as jnp
from jax import lax
from jax.experimental import pallas as pl
from jax.experimental.pallas import tpu as pltpu

F32 = jnp.float32
BF16 = jnp.bfloat16

D_MODEL = 1024
D_FF = 2816
HEAD_DIM = 64
HEADS = 8
GROUP_W = HEADS * HEAD_DIM
PAIR_W = 2 * HEAD_DIM
QKV_W = 3 * GROUP_W
A_GROUPS = ((128, 1), (512, 4), (2048, 16))
ROPE_THETA = 10000.0
B_WIDTH = 2048
B_GROUPS = 16
B_CHUNK = 128
EPS = 1e-6
NEG = -1e30
LOG2_E = 1.4426950408889634
LN_2 = 0.6931471805599453
Q_SCALE = HEAD_DIM ** -0.5 * LOG2_E

SEQ = 2048
BATCH = 16
DEC_BATCH = 32
DEC_SEQ = 8
PAST_LEN = 16384

V7X_VMEM_LIMIT_BYTES = 56 * 1024 * 1024
V7X_MXU_DEPTH = 256


def _dot(a, b):
    return jnp.dot(a, b, preferred_element_type=F32)


def _dot_nt(a, b):
    return lax.dot_general(a, b, (((1,), (1,)), ((), ())), preferred_element_type=F32)


def _rms(x, g):
    ms = jnp.mean(x * x, axis=-1, keepdims=True)
    return x * lax.rsqrt(ms + EPS) * g


def _gelu_tanh(x):
    cdf = 0.5 * (1.0 + jnp.tanh(0.7978845608028654 * (x + 0.044715 * (x * x * x))))
    return x * cdf


def _params(n_axes):
    return pltpu.CompilerParams(
        dimension_semantics=("arbitrary",) * n_axes,
        vmem_limit_bytes=V7X_VMEM_LIMIT_BYTES,
    )


def _resident(shape, index=None):
    if index is None:
        index = (0,) * len(shape)
    return pl.BlockSpec(shape, lambda *_: index, pipeline_mode=pl.Buffered(1))


FF_CHUNK = 256


def _ffn_kernel(*refs, with_proj):
    if with_proj:
        x_ref, m_ref, wo_ref, g_ref, win_ref, wout_ref, o_ref, h_ref = refs
        x = x_ref[...] + _dot(m_ref[...].astype(BF16), wo_ref[...])
    else:
        x_ref, g_ref, win_ref, wout_ref, o_ref, h_ref = refs
        x = x_ref[...]
    xn = _rms(x, g_ref[...]).astype(BF16)
    for c in range(D_FF // FF_CHUNK):
        lo = c * FF_CHUNK
        gate = _dot(xn, win_ref[:, lo:lo + FF_CHUNK])
        up = _dot(xn, win_ref[:, D_FF + lo:D_FF + lo + FF_CHUNK])
        act = gate * (1.0 / (1.0 + jnp.exp(-gate)))
        h_ref[:, lo:lo + FF_CHUNK] = (act * up).astype(BF16)
    o_ref[...] = x + 0.5 * _dot(h_ref[...], wout_ref[...])


def _ffn(x, g, w_in, w_out, layer, step, tm, proj=None):
    n = x.shape[0]
    x_spec = pl.BlockSpec((tm, D_MODEL), lambda i: (i, 0))
    in_specs = [x_spec]
    args = [x]
    if proj is not None:
        m, w_o = proj
        in_specs += [pl.BlockSpec((tm, GROUP_W), lambda i: (i, 0)), _resident((GROUP_W, D_MODEL))]
        args += [m, w_o]
    in_specs += [
        _resident((1, D_MODEL)),
        _resident((None, None, D_MODEL, 2 * D_FF), (layer, step, 0, 0)),
        _resident((None, None, D_FF, D_MODEL), (layer, step, 0, 0)),
    ]
    args += [g, w_in, w_out]
    return pl.pallas_call(
        functools.partial(_ffn_kernel, with_proj=proj is not None),
        out_shape=jax.ShapeDtypeStruct((n, D_MODEL), F32),
        grid=(n // tm,),
        in_specs=in_specs,
        out_specs=x_spec,
        scratch_shapes=[pltpu.VMEM((tm, D_FF), BF16)],
        compiler_params=_params(1),
        name="ffn",
    )(*args)


QK_BLOCK = 256
QKV_TM = 1024


def _qkv_kernel(x_ref, g_ref, w_ref, gain_ref, cos_ref, sin_ref, bd_ref, *out_refs, parts):
    xn = _rms(x_ref[...], g_ref[...]).astype(BF16)
    cos = cos_ref[...]
    sin = sin_ref[...]
    bd = bd_ref[...]

    def norm_rot(y, gain):
        ms = _dot((y * y).astype(BF16), bd)
        yn = y * lax.rsqrt(ms + EPS) * gain
        partner = jnp.concatenate(
            [pltpu.roll(yn[:, lo:lo + PAIR_W], PAIR_W // 2, 1)
             for lo in range(0, QK_BLOCK, PAIR_W)], axis=1)
        return yn * cos + partner * sin

    for part, o_ref in zip(parts, out_refs):
        if part == "q":
            for c0 in range(0, QKV_W, GROUP_W):
                y = _dot(xn, w_ref[:, c0:c0 + GROUP_W])
                for o0 in range(0, GROUP_W, QK_BLOCK):
                    o_ref[:, c0 + o0:c0 + o0 + QK_BLOCK] = norm_rot(
                        y[:, o0:o0 + QK_BLOCK], gain_ref[0:1, :])
            continue
        c0 = part * GROUP_W
        yk = _dot(xn, w_ref[:, QKV_W + c0:QKV_W + c0 + GROUP_W])
        for o0 in range(0, GROUP_W, QK_BLOCK):
            o_ref[:, o0:o0 + QK_BLOCK] = norm_rot(yk[:, o0:o0 + QK_BLOCK], gain_ref[1:2, :])
        o_ref[:, GROUP_W:] = _dot(xn, w_ref[:, 2 * QKV_W + c0:2 * QKV_W + c0 + GROUP_W])


def _qkv(x, g, w, gain, cos, sin, bd, tm, parts):
    n = x.shape[0]
    pos_blocks = cos.shape[0] // tm
    widths = [QKV_W if part == "q" else 2 * GROUP_W for part in parts]
    return pl.pallas_call(
        functools.partial(_qkv_kernel, parts=parts),
        out_shape=tuple(jax.ShapeDtypeStruct((n, wd), F32) for wd in widths),
        grid=(n // tm,),
        in_specs=[
            pl.BlockSpec((tm, D_MODEL), lambda i: (i, 0)),
            _resident((1, D_MODEL)),
            _resident((D_MODEL, 3 * QKV_W)),
            _resident((2, QK_BLOCK)),
            pl.BlockSpec((tm, QK_BLOCK), lambda i: (i % pos_blocks, 0)),
            pl.BlockSpec((tm, QK_BLOCK), lambda i: (i % pos_blocks, 0)),
            _resident((QK_BLOCK, QK_BLOCK)),
        ],
        out_specs=tuple(pl.BlockSpec((tm, wd), lambda i: (i, 0)) for wd in widths),
        compiler_params=_params(1),
        name="qkv",
    )(x, g, w, gain, cos, sin, bd)


TAIL_TM = 1024


def _tails_kernel(x_ref, g_ref, wt_ref, gain_ref, cos_ref, sin_ref, t0_ref, t1_ref, t2_ref,
                  *, tiles_per_seq):
    tm = x_ref.shape[0]
    half = HEAD_DIM // 2
    xn = _rms(x_ref[...], g_ref[...]).astype(BF16)
    gain = gain_ref[...]
    is_last_tile = (pl.program_id(0) % tiles_per_seq) == tiles_per_seq - 1

    def put_tail(g, t_ref):
        keep = t_ref.shape[-1]
        lo = tm - keep
        kv = _dot_nt(wt_ref[g], xn[lo:, :])
        cos = cos_ref[:, lo:]
        sin = sin_ref[:, lo:]
        for h in range(HEADS):
            y = kv[h * HEAD_DIM:(h + 1) * HEAD_DIM, :]
            ms = jnp.mean(y * y, axis=0, keepdims=True)
            yn = y * lax.rsqrt(ms + EPS) * gain
            x1, x2 = yn[:half], yn[half:]
            t_ref[0, h * HEAD_DIM:h * HEAD_DIM + half, :] = x1 * cos - x2 * sin
            t_ref[0, h * HEAD_DIM + half:(h + 1) * HEAD_DIM, :] = x2 * cos + x1 * sin
        t_ref[1] = kv[GROUP_W:]

    for g, t_ref in enumerate((t0_ref, t1_ref, t2_ref)):
        if A_GROUPS[g][0] >= SEQ:
            put_tail(g, t_ref)
        else:
            pl.when(is_last_tile)(functools.partial(put_tail, g, t_ref))


def _tails(x, g, wt, gain_col, cos_t, sin_t):
    n = x.shape[0]
    tm = TAIL_TM
    pos_blocks = SEQ // tm
    out_shape, out_specs = [], []
    for window, _ in A_GROUPS:
        keep = min(window, SEQ)
        assert keep <= tm or keep == SEQ
        out_shape.append(jax.ShapeDtypeStruct((n // SEQ, 2, GROUP_W, keep), F32))
        if keep > tm:
            index = lambda i: (i // pos_blocks, 0, 0, i % pos_blocks)
        else:
            index = lambda i: (i // pos_blocks, 0, 0, 0)
        out_specs.append(pl.BlockSpec((None, 2, GROUP_W, min(keep, tm)), index))
    return pl.pallas_call(
        functools.partial(_tails_kernel, tiles_per_seq=pos_blocks),
        out_shape=tuple(out_shape),
        grid=(n // tm,),
        in_specs=[
            pl.BlockSpec((tm, D_MODEL), lambda i: (i, 0)),
            _resident((1, D_MODEL)),
            _resident((len(A_GROUPS), 2 * GROUP_W, D_MODEL)),
            _resident((HEAD_DIM, 1)),
            pl.BlockSpec((HEAD_DIM // 2, tm), lambda i: (0, i % pos_blocks)),
            pl.BlockSpec((HEAD_DIM // 2, tm), lambda i: (0, i % pos_blocks)),
        ],
        out_specs=tuple(out_specs),
        compiler_params=_params(1),
        name="kv_tails",
    )(x, g, wt, gain_col, cos_t, sin_t)


BLK = 128
SPAN = 4


def _attn_kernel(q0_ref, q1_ref, q2_ref, k0_ref, v0_ref, k1_ref, v1_ref, k2_ref, v2_ref,
                 o_ref, og0, og1, og2, lg0, lg1, lg2, kmask_ref, qsel_ref):
    lane = lax.broadcasted_iota(jnp.int32, (BLK, PAIR_W), 1)
    head_a = lane < HEAD_DIM
    lane_ov = lax.broadcasted_iota(jnp.int32, (BLK, 2 * PAIR_W), 1)
    head_a2 = (lane_ov & (PAIR_W - 1)) < HEAD_DIM
    row2 = lax.broadcasted_iota(jnp.int32, (2 * BLK, PAIR_W), 0)
    lane2 = lax.broadcasted_iota(jnp.int32, (2 * BLK, PAIR_W), 1)
    q_keep = (row2 >> 7) == ((lane2 >> 5) & 1)
    key = lax.broadcasted_iota(jnp.int32, (2 * BLK, BLK), 0)
    qi = lax.broadcasted_iota(jnp.int32, (2 * BLK, BLK), 1)
    jk = key & (BLK - 1)
    valid = ((key < BLK) & (jk >= qi)) | ((key >= BLK) & (jk <= qi))
    kmask_ref[...] = jnp.where(valid, 0.0, NEG).astype(BF16)
    qsel_ref[...] = jnp.where(jk == qi, 1.0, 0.0).astype(BF16)

    def one_block(q, kb, vb, kmask):
        q2 = jnp.where(q_keep, jnp.concatenate([q, q], axis=0), 0.0).astype(BF16)
        s = _dot_nt(jnp.concatenate([q2, qsel_ref[...]], axis=1),
                    jnp.concatenate([kb, kmask], axis=1))
        m = jnp.max(s, axis=1, keepdims=True)
        p = jnp.exp2(s - m).astype(BF16)
        ones = jnp.ones((vb.shape[0], PAIR_W), BF16)
        pv = _dot(p, jnp.concatenate([vb, ones], axis=1))
        pv = jnp.where(head_a2, pv[:BLK], pv[BLK:])
        m = jnp.where(head_a, m[:BLK], m[BLK:])
        l = pv[:, PAIR_W:]
        return pv[:, :PAIR_W] / l, m * LN_2 + jnp.log(l)

    def span(refs, dil, r, n0, count, has_prev):
        q_ref, k_ref, v_ref, og_ref, lg_ref = refs

        def rows(n):
            return pl.ds(r + (n0 + n) * (dil * BLK), BLK, stride=dil)

        first = -1 if has_prev else 0
        kb = {n: k_ref[rows(n), :].astype(BF16) for n in range(first, count)}
        vb = {n: v_ref[rows(n), :].astype(BF16) for n in range(first, count)}
        for n in range(count):
            q = q_ref[rows(n), :]
            if n - 1 in kb:
                kcat = jnp.concatenate([kb[n - 1], kb[n]], axis=0)
                vcat = jnp.concatenate([vb[n - 1], vb[n]], axis=0)
                o, lse = one_block(q, kcat, vcat, kmask_ref[...])
            else:
                o, lse = one_block(q, kb[n], vb[n], kmask_ref[BLK:, :])
            og_ref[rows(n), :] = o
            lg_ref[rows(n), :] = lse

    refs0 = (q0_ref, k0_ref, v0_ref, og0, lg0)
    refs1 = (q1_ref, k1_ref, v1_ref, og1, lg1)
    refs2 = (q2_ref, k2_ref, v2_ref, og2, lg2)
    dil0, dil1, dil2 = (d for _, d in A_GROUPS)
    blocks0, blocks1, blocks2 = (SEQ // (d * BLK) for _, d in A_GROUPS)
    assert blocks0 % SPAN == 0 and blocks1 == SPAN and blocks2 == 1 and dil2 % SPAN == 0

    for it in range(blocks0 // SPAN):
        span(refs0, dil0, 0, it * SPAN, SPAN, it > 0)
    for r in range(dil1):
        span(refs1, dil1, r, 0, SPAN, False)
    for r in range(dil2):
        span(refs2, dil2, r, 0, 1, False)

    def merge(i, carry):
        rows = pl.ds(pl.multiple_of(i * BLK, BLK), BLK)
        l0, l1, l2 = lg0[rows, :], lg1[rows, :], lg2[rows, :]
        mx = jnp.maximum(jnp.maximum(l0, l1), l2)
        e0, e1, e2 = jnp.exp(l0 - mx), jnp.exp(l1 - mx), jnp.exp(l2 - mx)
        num = e0 * og0[rows, :] + e1 * og1[rows, :] + e2 * og2[rows, :]
        o_ref[rows, :] = (num / (e0 + e1 + e2)).astype(o_ref.dtype)
        return carry

    lax.fori_loop(0, SEQ // BLK, merge, 0, unroll=4)


def _attn_prompt(q, kv0, kv1, kv2):
    n = q.shape[0]
    pairs = GROUP_W // PAIR_W
    blk = (SEQ, PAIR_W)
    q_specs = [pl.BlockSpec(blk, functools.partial(lambda b, hp, g: (b, g * pairs + hp), g=g))
               for g in range(3)]
    k_spec = pl.BlockSpec(blk, lambda b, hp: (b, hp))
    v_spec = pl.BlockSpec(blk, lambda b, hp: (b, pairs + hp))
    scratch = [pltpu.VMEM(blk, F32) for _ in range(6)] + [pltpu.VMEM((2 * BLK, BLK), BF16)] * 2
    return pl.pallas_call(
        _attn_kernel,
        out_shape=jax.ShapeDtypeStruct((n, GROUP_W), BF16),
        grid=(n // SEQ, pairs),
        in_specs=q_specs + [k_spec, v_spec, k_spec, v_spec, k_spec, v_spec],
        out_specs=pl.BlockSpec(blk, lambda b, hp: (b, hp)),
        scratch_shapes=scratch,
        compiler_params=_params(2),
        name="attn_prompt",
    )(q, q, q, kv0, kv0, kv1, kv1, kv2, kv2)


NEW_PAD = 128
QROWS = HEADS * DEC_SEQ


def _attn_sample_kernel(q_ref, n0_ref, n1_ref, n2_ref, c0_ref, c1_ref, c2_ref, o_ref):
    new_refs = (n0_ref, n1_ref, n2_ref)
    cache_refs = (c0_ref, c1_ref, c2_ref)
    rowq = lax.broadcasted_iota(jnp.int32, (QROWS, GROUP_W), 0)
    laneq = lax.broadcasted_iota(jnp.int32, (QROWS, GROUP_W), 1)
    own_head = (rowq >> 3) == (laneq >> 6)
    outs, lses = [], []
    for g, (window, dil) in enumerate(A_GROUPS):
        q = q_ref[:, g * GROUP_W:(g + 1) * GROUP_W]
        qbd = jnp.where(own_head, jnp.concatenate([q] * HEADS, axis=0), 0.0).astype(BF16)
        k_old = cache_refs[g][0].astype(BF16)
        v_old = cache_refs[g][1].astype(BF16)
        new = jnp.concatenate(
            [new_refs[g][...], jnp.zeros((NEW_PAD - DEC_SEQ, 2 * GROUP_W), F32)], axis=0)
        k_new = new[:, :GROUP_W].astype(BF16)
        v_new = new[:, GROUP_W:].astype(BF16)

        key = lax.broadcasted_iota(jnp.int32, (QROWS, window), 1)
        j = lax.broadcasted_iota(jnp.int32, (QROWS, window), 0) & (DEC_SEQ - 1)
        back = window + j - key
        valid_old = ((back & (dil - 1)) == 0) & (key >= j)
        jn = lax.broadcasted_iota(jnp.int32, (QROWS, NEW_PAD), 1)
        jq = lax.broadcasted_iota(jnp.int32, (QROWS, NEW_PAD), 0) & (DEC_SEQ - 1)
        valid_new = (jn <= jq) & (((jq - jn) & (dil - 1)) == 0)

        s_old = jnp.where(valid_old, _dot(qbd, k_old), NEG)
        s_new = jnp.where(valid_new, _dot_nt(qbd, k_new), NEG)
        m = jnp.maximum(jnp.max(s_old, axis=1, keepdims=True),
                        jnp.max(s_new, axis=1, keepdims=True))
        p_old = jnp.exp2(s_old - m)
        p_new = jnp.exp2(s_new - m)
        l = jnp.sum(p_old, axis=1, keepdims=True) + jnp.sum(p_new, axis=1, keepdims=True)
        o = (_dot_nt(p_old.astype(BF16), v_old) + _dot(p_new.astype(BF16), v_new)) / l
        lse = m * LN_2 + jnp.log(l)
        om = jnp.where(own_head, o, 0.0)
        lm = jnp.where(own_head, lse, 0.0)
        og, lg = om[0:DEC_SEQ], lm[0:DEC_SEQ]
        for h in range(1, HEADS):
            og = og + om[h * DEC_SEQ:(h + 1) * DEC_SEQ]
            lg = lg + lm[h * DEC_SEQ:(h + 1) * DEC_SEQ]
        outs.append(og)
        lses.append(lg)
    mx = jnp.maximum(jnp.maximum(lses[0], lses[1]), lses[2])
    e = [jnp.exp(x - mx) for x in lses]
    num = e[0] * outs[0] + e[1] * outs[1] + e[2] * outs[2]
    o_ref[...] = num / (e[0] + e[1] + e[2])


def _attn_sample(q, new0, new1, new2, c0, c1, c2):
    row_spec = lambda w: pl.BlockSpec((DEC_SEQ, w), lambda b: (b, 0))
    cache_spec = lambda c: pl.BlockSpec((None,) + c.shape[1:], lambda b: (b, 0, 0, 0))
    return pl.pallas_call(
        _attn_sample_kernel,
        out_shape=jax.ShapeDtypeStruct((DEC_BATCH * DEC_SEQ, GROUP_W), F32),
        grid=(DEC_BATCH,),
        in_specs=[
            row_spec(QKV_W), row_spec(2 * GROUP_W), row_spec(2 * GROUP_W), row_spec(2 * GROUP_W),
            cache_spec(c0), cache_spec(c1), cache_spec(c2),
        ],
        out_specs=row_spec(GROUP_W),
        compiler_params=_params(1),
        name="attn_sample",
    )(q, new0, new1, new2, c0, c1, c2)


UV_CHUNK = 512
B_GROUP_DIM = B_WIDTH // B_GROUPS


def _mixb_kernel(x_ref, g_ref, wuv_ref, vgain_ref, wsp_ref, bt_ref, wo_ref, *rest,
                 chunk, emit_v):
    if emit_v:
        o_ref, vout_ref, v_ref, h_ref = rest
    else:
        o_ref, v_ref, h_ref = rest
    tm = x_ref.shape[0]
    r = wsp_ref.shape[1]
    nsub = tm // r
    x = x_ref[...]
    xn = _rms(x, g_ref[...]).astype(BF16)
    ssq = jnp.zeros((tm, 1), F32)
    for c in range(B_WIDTH // UV_CHUNK):
        lo = c * UV_CHUNK
        v_raw = _dot(xn, wuv_ref[:, B_WIDTH + lo:B_WIDTH + lo + UV_CHUNK])
        for o in range(0, UV_CHUNK, B_GROUP_DIM):
            vv = _gelu_tanh(v_raw[:, o:o + B_GROUP_DIM])
            v_ref[:, lo + o:lo + o + B_GROUP_DIM] = vv
            ssq = ssq + jnp.sum(vv * vv, axis=-1, keepdims=True)
    inv = lax.rsqrt(ssq * (1.0 / B_WIDTH) + EPS)
    row = lax.broadcasted_iota(jnp.int32, (r, r), 0)
    col = lax.broadcasted_iota(jnp.int32, (r, r), 1)
    shift = chunk.bit_length() - 1
    keep = ((row >> shift) == (col >> shift)) & ((col & (chunk - 1)) <= (row & (chunk - 1)))
    groups_per_chunk = UV_CHUNK // B_GROUP_DIM
    stack = max(1, V7X_MXU_DEPTH // r)
    for c in range(B_WIDTH // UV_CHUNK):
        u_raw = _dot(xn, wuv_ref[:, c * UV_CHUNK:(c + 1) * UV_CHUNK])
        for gi0 in range(0, groups_per_chunk, stack):
            gis = range(gi0, gi0 + stack)
            vbs, ws = [], []
            for gi in gis:
                g = c * groups_per_chunk + gi
                lanes = slice(g * B_GROUP_DIM, (g + 1) * B_GROUP_DIM)
                vg = v_ref[:, lanes] * inv * vgain_ref[:, lanes]
                if emit_v:
                    vout_ref[:, lanes] = vg
                vb = vg.astype(BF16)
                if nsub > 1:
                    vb = jnp.concatenate([vb[s * r:(s + 1) * r] for s in range(nsub)], axis=1)
                vbs.append(vb)
                ws.append(jnp.where(keep, wsp_ref[g], 0.0).astype(BF16))
            if stack == 1:
                mixed_all = _dot(ws[0], vbs[0])
            else:
                zero = jnp.zeros((r, r), BF16)
                w_bd = jnp.concatenate(
                    [jnp.concatenate([ws[i] if i == j else zero for j in range(stack)], axis=1)
                     for i in range(stack)], axis=0)
                mixed_all = _dot(w_bd, jnp.concatenate(vbs, axis=0))
            for i, gi in enumerate(gis):
                g = c * groups_per_chunk + gi
                lanes = slice(g * B_GROUP_DIM, (g + 1) * B_GROUP_DIM)
                mixed = mixed_all[i * r:(i + 1) * r] + bt_ref[:, g:g + 1]
                ug = _gelu_tanh(u_raw[:, gi * B_GROUP_DIM:(gi + 1) * B_GROUP_DIM])
                for s in range(nsub):
                    rows = slice(s * r, (s + 1) * r)
                    h_ref[rows, lanes] = (
                        ug[rows] * mixed[:, s * B_GROUP_DIM:(s + 1) * B_GROUP_DIM]
                    ).astype(BF16)
    o_ref[...] = x + _dot(h_ref[...], wo_ref[...])


def _mixb(x, g, w_uv, v_gain, w_sp, b_t, w_o, tm, chunk, emit_v):
    n = x.shape[0]
    r = w_sp.shape[1]
    x_spec = pl.BlockSpec((tm, D_MODEL), lambda i: (i, 0))
    out_shape = [jax.ShapeDtypeStruct((n, D_MODEL), F32)]
    out_specs = [x_spec]
    if emit_v:
        out_shape.append(jax.ShapeDtypeStruct((n, B_WIDTH), F32))
        out_specs.append(pl.BlockSpec((tm, B_WIDTH), lambda i: (i, 0)))
    return pl.pallas_call(
        functools.partial(_mixb_kernel, chunk=chunk, emit_v=emit_v),
        out_shape=tuple(out_shape),
        grid=(n // tm,),
        in_specs=[
            x_spec,
            _resident((1, D_MODEL)),
            _resident((D_MODEL, 2 * B_WIDTH)),
            _resident((1, B_WIDTH)),
            _resident((B_GROUPS, r, r)),
            _resident((r, B_GROUPS)),
            _resident((B_WIDTH, D_MODEL)),
        ],
        out_specs=tuple(out_specs),
        scratch_shapes=[
            pltpu.VMEM((tm, B_WIDTH), F32),
            pltpu.VMEM((tm, B_WIDTH), BF16),
        ],
        compiler_params=_params(1),
        name="spatial_gating",
    )(x, g, w_uv, v_gain, w_sp, b_t, w_o)


def _rope_angles(pos):
    half = HEAD_DIM // 2
    inv = ROPE_THETA ** (-jnp.arange(half, dtype=F32) / half)
    ang = pos.astype(F32)[:, None] * inv[None, :]
    return jnp.cos(ang), jnp.sin(ang)


def _pair_order(width):
    half = HEAD_DIM // 2
    n = jnp.arange(width)
    lane = n % PAIR_W
    head, hi = (lane // half) % 2, lane // HEAD_DIM
    return (n // PAIR_W) * PAIR_W + head * HEAD_DIM + hi * half + lane % half


def _rope_tables(pos):
    cos, sin = _rope_angles(pos)
    cos_t = jnp.tile(cos, (1, QK_BLOCK // (HEAD_DIM // 2)))
    sin_t = jnp.tile(jnp.concatenate([-sin, -sin, sin, sin], axis=-1), (1, QK_BLOCK // PAIR_W))
    return cos_t, sin_t


def _feature_major(cache):
    _, b, length, two, h, hd = cache.shape
    return jnp.transpose(cache[0], (0, 2, 3, 4, 1)).reshape(b, two, h * hd, length)


def _position_major(tail):
    b, two, _, length = tail.shape
    t = tail.reshape(b, two, HEADS, HEAD_DIM, length)
    return jnp.transpose(t, (0, 4, 1, 2, 3))[None]


def kernel(x_prompt, x_sample, cache_kv_w128, cache_kv_w512, cache_kv_w2048, norm_g,
           w_ffn_in, w_ffn_out, w_qkv_a, qk_gain_a, w_o_a, w_uv_b, v_gain_b, w_sp_b, b_sp_b,
           w_o_b):
    xp = x_prompt.reshape(BATCH * SEQ, D_MODEL)
    xs = x_sample.reshape(DEC_BATCH * DEC_SEQ, D_MODEL)
    n_s = DEC_BATCH * DEC_SEQ
    tm_p = 1024

    w_in = w_ffn_in.astype(BF16)
    w_out = w_ffn_out.astype(BF16)
    order = _pair_order(QKV_W)
    w_qkv = jnp.concatenate(
        [w_qkv_a[0][:, :QKV_W][:, order], w_qkv_a[0][:, QKV_W:2 * QKV_W][:, order],
         w_qkv_a[0][:, 2 * QKV_W:]], axis=1).astype(BF16)
    w_oa = w_o_a[0].astype(BF16)
    w_uv = w_uv_b[0].astype(BF16)
    w_ob = w_o_b[0].astype(BF16)
    gn = norm_g.reshape(2, 3, 1, D_MODEL)

    natural = order[:QK_BLOCK]
    head_id = natural // HEAD_DIM
    bd = jnp.where(head_id[:, None] == head_id[None, :], 1.0 / HEAD_DIM, 0.0).astype(BF16)
    gain = qk_gain_a[0][:, natural % HEAD_DIM] * jnp.array([[Q_SCALE], [1.0]], F32)
    cos_p, sin_p = _rope_tables(jnp.arange(SEQ))
    cos_s, sin_s = _rope_tables(PAST_LEN + jnp.arange(DEC_SEQ))
    cos_s = jnp.tile(cos_s, (DEC_BATCH, 1))
    sin_s = jnp.tile(sin_s, (DEC_BATCH, 1))

    xp = _ffn(xp, gn[0, 0], w_in, w_out, 0, 0, tm_p)
    xs = _ffn(xs, gn[0, 0], w_in, w_out, 0, 0, n_s)

    qp, kvp0 = _qkv(xp, gn[0, 1], w_qkv, gain, cos_p, sin_p, bd, QKV_TM, ("q", 0))
    kvp1, kvp2 = _qkv(xp, gn[0, 1], w_qkv, gain, cos_p, sin_p, bd, QKV_TM, (1, 2))
    qs, kvs0, kvs1, kvs2 = _qkv(xs, gn[0, 1], w_qkv, gain, cos_s, sin_s, bd, n_s, ("q", 0, 1, 2))
    qs = qs[:, order]
    kvs0, kvs1, kvs2 = (
        jnp.concatenate([kv[:, :GROUP_W][:, order[:GROUP_W]], kv[:, GROUP_W:]], axis=1)
        for kv in (kvs0, kvs1, kvs2))

    groups = len(A_GROUPS)
    w_kv_t = jnp.transpose(
        w_qkv_a[0][:, QKV_W:].reshape(D_MODEL, 2, groups, GROUP_W), (2, 1, 3, 0)
    ).reshape(groups, 2 * GROUP_W, D_MODEL).astype(BF16)
    cos_a, sin_a = _rope_angles(jnp.arange(SEQ))
    tail0, tail1, tail2 = _tails(xp, gn[0, 1], w_kv_t, qk_gain_a[0, 1].reshape(HEAD_DIM, 1),
                                 cos_a.T, sin_a.T)

    mp = _attn_prompt(qp, kvp0, kvp1, kvp2)
    ms = _attn_sample(qs, kvs0, kvs1, kvs2, _feature_major(cache_kv_w128),
                      _feature_major(cache_kv_w512), _feature_major(cache_kv_w2048))

    xp = _ffn(xp, gn[0, 2], w_in, w_out, 0, 1, tm_p, proj=(mp, w_oa))
    xs = _ffn(xs, gn[0, 2], w_in, w_out, 0, 1, n_s, proj=(ms, w_oa))

    xp = _ffn(xp, gn[1, 0], w_in, w_out, 1, 0, tm_p)
    xs = _ffn(xs, gn[1, 0], w_in, w_out, 1, 0, n_s)

    vgain = v_gain_b[0].reshape(1, B_WIDTH)
    (xp,) = _mixb(xp, gn[1, 1], w_uv, vgain, w_sp_b[0], b_sp_b[0].T, w_ob,
                  tm=1024, chunk=B_CHUNK, emit_v=False)
    w_sp_s = jnp.tile(w_sp_b[0][:, :DEC_SEQ, :DEC_SEQ], (1, DEC_BATCH, DEC_BATCH))
    b_t_s = jnp.tile(b_sp_b[0][:, :DEC_SEQ].T, (DEC_BATCH, 1))
    xs, v_new = _mixb(xs, gn[1, 1], w_uv, vgain, w_sp_s, b_t_s, w_ob,
                      tm=n_s, chunk=DEC_SEQ, emit_v=True)

    xp = _ffn(xp, gn[1, 2], w_in, w_out, 1, 1, tm_p)
    xs = _ffn(xs, gn[1, 2], w_in, w_out, 1, 1, n_s)

    def rows(kv):
        return kv.reshape(1, DEC_BATCH, DEC_SEQ, 2, HEADS, HEAD_DIM)

    return (
        xp.reshape(BATCH, SEQ, D_MODEL),
        xs.reshape(DEC_BATCH, DEC_SEQ, D_MODEL),
        _position_major(tail0), _position_major(tail1), _position_major(tail2),
        rows(kvs0), rows(kvs1), rows(kvs2),
        v_new.reshape(1, DEC_BATCH, DEC_SEQ, B_WIDTH),
    )
```

```python
import functools

import jax
import jax.numpy as jnp
from jax import lax
from jax.experimental import pallas as pl
from jax.experimental.pallas import tpu as pltpu

F32 = jnp.float32
BF16 = jnp.bfloat16

D_MODEL = 1024
D_FF = 2816
HEAD_DIM = 64
HEADS = 8
GROUP_W = HEADS * HEAD_DIM
PAIR_W = 2 * HEAD_DIM
QKV_W = 3 * GROUP_W
A_GROUPS = ((128, 1), (512, 4), (2048, 16))
ROPE_THETA = 10000.0
B_WIDTH = 2048
B_GROUPS = 16
B_CHUNK = 128
EPS = 1e-6
NEG = -1e30
LOG2_E = 1.4426950408889634
LN_2 = 0.6931471805599453
Q_SCALE = HEAD_DIM ** -0.5 * LOG2_E

SEQ = 2048
BATCH = 16
DEC_BATCH = 32
DEC_SEQ = 8
PAST_LEN = 16384

V7X_VMEM_LIMIT_BYTES = 56 * 1024 * 1024
V7X_MXU_DEPTH = 256


def _log2(n):
    assert n > 0 and n & (n - 1) == 0, n
    return n.bit_length() - 1


def _dot(a, b):
    return jnp.dot(a, b, preferred_element_type=F32)


def _dot_nt(a, b):
    return lax.dot_general(a, b, (((1,), (1,)), ((), ())), preferred_element_type=F32)


def _rms(x, g):
    ms = jnp.mean(x * x, axis=-1, keepdims=True)
    return x * lax.rsqrt(ms + EPS) * g


def _gelu_tanh(x):
    cdf = 0.5 * (1.0 + jnp.tanh(0.7978845608028654 * (x + 0.044715 * (x * x * x))))
    return x * cdf


def _params(n_axes):
    return pltpu.CompilerParams(
        dimension_semantics=("arbitrary",) * n_axes,
        vmem_limit_bytes=V7X_VMEM_LIMIT_BYTES,
    )


def _resident(shape, index=None):
    if index is None:
        index = (0,) * len(shape)
    return pl.BlockSpec(shape, lambda *_: index, pipeline_mode=pl.Buffered(1))


FF_CHUNK = 256


def _ffn_kernel(*refs, with_proj):
    if with_proj:
        x_ref, m_ref, wo_ref, g_ref, win_ref, wout_ref, o_ref, h_ref = refs
        x = x_ref[...] + _dot(m_ref[...].astype(BF16), wo_ref[...])
    else:
        x_ref, g_ref, win_ref, wout_ref, o_ref, h_ref = refs
        x = x_ref[...]
    xn = _rms(x, g_ref[...]).astype(BF16)
    for c in range(D_FF // FF_CHUNK):
        lo = c * FF_CHUNK
        gate = _dot(xn, win_ref[:, lo:lo + FF_CHUNK])
        up = _dot(xn, win_ref[:, D_FF + lo:D_FF + lo + FF_CHUNK])
        act = gate * (1.0 / (1.0 + jnp.exp(-gate)))
        h_ref[:, lo:lo + FF_CHUNK] = (act * up).astype(BF16)
    o_ref[...] = x + 0.5 * _dot(h_ref[...], wout_ref[...])


def _ffn(x, g, w_in, w_out, layer, step, tm, proj=None):
    n = x.shape[0]
    x_spec = pl.BlockSpec((tm, D_MODEL), lambda i: (i, 0))
    in_specs = [x_spec]
    args = [x]
    if proj is not None:
        m, w_o = proj
        in_specs += [pl.BlockSpec((tm, GROUP_W), lambda i: (i, 0)), _resident((GROUP_W, D_MODEL))]
        args += [m, w_o]
    in_specs += [
        _resident((1, D_MODEL)),
        _resident((None, None, D_MODEL, 2 * D_FF), (layer, step, 0, 0)),
        _resident((None, None, D_FF, D_MODEL), (layer, step, 0, 0)),
    ]
    args += [g, w_in, w_out]
    return pl.pallas_call(
        functools.partial(_ffn_kernel, with_proj=proj is not None),
        out_shape=jax.ShapeDtypeStruct((n, D_MODEL), F32),
        grid=(n // tm,),
        in_specs=in_specs,
        out_specs=x_spec,
        scratch_shapes=[pltpu.VMEM((tm, D_FF), BF16)],
        compiler_params=_params(1),
        name="ffn",
    )(*args)


QK_BLOCK = 256
QKV_TM = 1024


def _qkv_kernel(x_ref, g_ref, w_ref, gain_ref, cos_ref, sin_ref, bd_ref, *out_refs, parts):
    xn = _rms(x_ref[...], g_ref[...]).astype(BF16)
    cos = cos_ref[...]
    sin = sin_ref[...]
    bd = bd_ref[...]

    def norm_rot(y, gain):
        ms = _dot((y * y).astype(BF16), bd)
        yn = y * lax.rsqrt(ms + EPS) * gain
        partner = jnp.concatenate(
            [pltpu.roll(yn[:, lo:lo + PAIR_W], PAIR_W // 2, 1)
             for lo in range(0, QK_BLOCK, PAIR_W)], axis=1)
        return yn * cos + partner * sin

    for part, o_ref in zip(parts, out_refs):
        if part == "q":
            for c0 in range(0, QKV_W, GROUP_W):
                y = _dot(xn, w_ref[:, c0:c0 + GROUP_W])
                for o0 in range(0, GROUP_W, QK_BLOCK):
                    o_ref[:, c0 + o0:c0 + o0 + QK_BLOCK] = norm_rot(
                        y[:, o0:o0 + QK_BLOCK], gain_ref[0:1, :])
            continue
        c0 = part * GROUP_W
        yk = _dot(xn, w_ref[:, QKV_W + c0:QKV_W + c0 + GROUP_W])
        for o0 in range(0, GROUP_W, QK_BLOCK):
            o_ref[:, o0:o0 + QK_BLOCK] = norm_rot(yk[:, o0:o0 + QK_BLOCK], gain_ref[1:2, :])
        o_ref[:, GROUP_W:] = _dot(xn, w_ref[:, 2 * QKV_W + c0:2 * QKV_W + c0 + GROUP_W])


def _qkv(x, g, w, gain, cos, sin, bd, tm, parts):
    n = x.shape[0]
    pos_blocks = cos.shape[0] // tm
    widths = [QKV_W if part == "q" else 2 * GROUP_W for part in parts]
    return pl.pallas_call(
        functools.partial(_qkv_kernel, parts=parts),
        out_shape=tuple(jax.ShapeDtypeStruct((n, wd), F32) for wd in widths),
        grid=(n // tm,),
        in_specs=[
            pl.BlockSpec((tm, D_MODEL), lambda i: (i, 0)),
            _resident((1, D_MODEL)),
            _resident((D_MODEL, 3 * QKV_W)),
            _resident((2, QK_BLOCK)),
            pl.BlockSpec((tm, QK_BLOCK), lambda i: (i % pos_blocks, 0)),
            pl.BlockSpec((tm, QK_BLOCK), lambda i: (i % pos_blocks, 0)),
            _resident((QK_BLOCK, QK_BLOCK)),
        ],
        out_specs=tuple(pl.BlockSpec((tm, wd), lambda i: (i, 0)) for wd in widths),
        compiler_params=_params(1),
        name="qkv",
    )(x, g, w, gain, cos, sin, bd)


TAIL_TM = 1024


def _tails_kernel(x_ref, g_ref, wt_ref, gain_ref, cos_ref, sin_ref, t0_ref, t1_ref, t2_ref,
                  *, tiles_per_seq):
    tm = x_ref.shape[0]
    half = HEAD_DIM // 2
    xn = _rms(x_ref[...], g_ref[...]).astype(BF16)
    gain = gain_ref[...]
    is_last_tile = (pl.program_id(0) % tiles_per_seq) == tiles_per_seq - 1

    def put_tail(g, t_ref):
        keep = t_ref.shape[-1]
        lo = tm - keep
        kv = _dot_nt(wt_ref[g], xn[lo:, :])
        cos = cos_ref[:, lo:]
        sin = sin_ref[:, lo:]
        for h in range(HEADS):
            y = kv[h * HEAD_DIM:(h + 1) * HEAD_DIM, :]
            ms = jnp.mean(y * y, axis=0, keepdims=True)
            yn = y * lax.rsqrt(ms + EPS) * gain
            x1, x2 = yn[:half], yn[half:]
            t_ref[0, h * HEAD_DIM:h * HEAD_DIM + half, :] = x1 * cos - x2 * sin
            t_ref[0, h * HEAD_DIM + half:(h + 1) * HEAD_DIM, :] = x2 * cos + x1 * sin
        t_ref[1] = kv[GROUP_W:]

    for g, t_ref in enumerate((t0_ref, t1_ref, t2_ref)):
        if A_GROUPS[g][0] >= SEQ:
            put_tail(g, t_ref)
        else:
            pl.when(is_last_tile)(functools.partial(put_tail, g, t_ref))


def _tails(x, g, wt, gain_col, cos_t, sin_t):
    n = x.shape[0]
    tm = TAIL_TM
    pos_blocks = SEQ // tm
    out_shape, out_specs = [], []
    for window, _ in A_GROUPS:
        keep = min(window, SEQ)
        assert keep <= tm or keep == SEQ
        out_shape.append(jax.ShapeDtypeStruct((n // SEQ, 2, GROUP_W, keep), F32))
        if keep > tm:
            index = lambda i: (i // pos_blocks, 0, 0, i % pos_blocks)
        else:
            index = lambda i: (i // pos_blocks, 0, 0, 0)
        out_specs.append(pl.BlockSpec((None, 2, GROUP_W, min(keep, tm)), index))
    return pl.pallas_call(
        functools.partial(_tails_kernel, tiles_per_seq=pos_blocks),
        out_shape=tuple(out_shape),
        grid=(n // tm,),
        in_specs=[
            pl.BlockSpec((tm, D_MODEL), lambda i: (i, 0)),
            _resident((1, D_MODEL)),
            _resident((len(A_GROUPS), 2 * GROUP_W, D_MODEL)),
            _resident((HEAD_DIM, 1)),
            pl.BlockSpec((HEAD_DIM // 2, tm), lambda i: (0, i % pos_blocks)),
            pl.BlockSpec((HEAD_DIM // 2, tm), lambda i: (0, i % pos_blocks)),
        ],
        out_specs=tuple(out_specs),
        compiler_params=_params(1),
        name="kv_tails",
    )(x, g, wt, gain_col, cos_t, sin_t)


BLK = 128
SPAN = 4


def _attn_kernel(q0_ref, q1_ref, q2_ref, k0_ref, v0_ref, k1_ref, v1_ref, k2_ref, v2_ref,
                 o_ref, og0, og1, og2, lg0, lg1, lg2, kmask_ref, qsel_ref):
    lane = lax.broadcasted_iota(jnp.int32, (BLK, PAIR_W), 1)
    head_a = lane < HEAD_DIM
    lane_ov = lax.broadcasted_iota(jnp.int32, (BLK, 2 * PAIR_W), 1)
    head_a2 = (lane_ov & (PAIR_W - 1)) < HEAD_DIM
    row2 = lax.broadcasted_iota(jnp.int32, (2 * BLK, PAIR_W), 0)
    lane2 = lax.broadcasted_iota(jnp.int32, (2 * BLK, PAIR_W), 1)
    q_keep = (row2 >> _log2(BLK)) == ((lane2 >> _log2(HEAD_DIM // 2)) & 1)
    key = lax.broadcasted_iota(jnp.int32, (2 * BLK, BLK), 0)
    qi = lax.broadcasted_iota(jnp.int32, (2 * BLK, BLK), 1)
    jk = key & (BLK - 1)
    valid = ((key < BLK) & (jk >= qi)) | ((key >= BLK) & (jk <= qi))
    kmask_ref[...] = jnp.where(valid, 0.0, NEG).astype(BF16)
    qsel_ref[...] = jnp.where(jk == qi, 1.0, 0.0).astype(BF16)

    def one_block(q, kb, vb, kmask):
        q2 = jnp.where(q_keep, jnp.concatenate([q, q], axis=0), 0.0).astype(BF16)
        s = _dot_nt(jnp.concatenate([q2, qsel_ref[...]], axis=1),
                    jnp.concatenate([kb, kmask], axis=1))
        m = jnp.max(s, axis=1, keepdims=True)
        p = jnp.exp2(s - m).astype(BF16)
        ones = jnp.ones((vb.shape[0], PAIR_W), BF16)
        pv = _dot(p, jnp.concatenate([vb, ones], axis=1))
        pv = jnp.where(head_a2, pv[:BLK], pv[BLK:])
        m = jnp.where(head_a, m[:BLK], m[BLK:])
        l = pv[:, PAIR_W:]
        return pv[:, :PAIR_W] / l, m * LN_2 + jnp.log(l)

    def span(refs, dil, r, n0, count, has_prev):
        q_ref, k_ref, v_ref, og_ref, lg_ref = refs

        def rows(n):
            return pl.ds(r + (n0 + n) * (dil * BLK), BLK, stride=dil)

        first = -1 if has_prev else 0
        kb = {n: k_ref[rows(n), :].astype(BF16) for n in range(first, count)}
        vb = {n: v_ref[rows(n), :].astype(BF16) for n in range(first, count)}
        for n in range(count):
            q = q_ref[rows(n), :]
            if n - 1 in kb:
                kcat = jnp.concatenate([kb[n - 1], kb[n]], axis=0)
                vcat = jnp.concatenate([vb[n - 1], vb[n]], axis=0)
                o, lse = one_block(q, kcat, vcat, kmask_ref[...])
            else:
                o, lse = one_block(q, kb[n], vb[n], kmask_ref[BLK:, :])
            og_ref[rows(n), :] = o
            lg_ref[rows(n), :] = lse

    refs0 = (q0_ref, k0_ref, v0_ref, og0, lg0)
    refs1 = (q1_ref, k1_ref, v1_ref, og1, lg1)
    refs2 = (q2_ref, k2_ref, v2_ref, og2, lg2)
    dil0, dil1, dil2 = (d for _, d in A_GROUPS)
    blocks0, blocks1, blocks2 = (SEQ // (d * BLK) for _, d in A_GROUPS)
    assert blocks0 % SPAN == 0 and blocks1 == SPAN and blocks2 == 1 and dil2 % SPAN == 0

    for it in range(blocks0 // SPAN):
        span(refs0, dil0, 0, it * SPAN, SPAN, it > 0)
    for r in range(dil1):
        span(refs1, dil1, r, 0, SPAN, False)
    for r in range(dil2):
        span(refs2, dil2, r, 0, 1, False)

    def merge(i, carry):
        rows = pl.ds(pl.multiple_of(i * BLK, BLK), BLK)
        l0, l1, l2 = lg0[rows, :], lg1[rows, :], lg2[rows, :]
        mx = jnp.maximum(jnp.maximum(l0, l1), l2)
        e0, e1, e2 = jnp.exp(l0 - mx), jnp.exp(l1 - mx), jnp.exp(l2 - mx)
        num = e0 * og0[rows, :] + e1 * og1[rows, :] + e2 * og2[rows, :]
        o_ref[rows, :] = (num / (e0 + e1 + e2)).astype(o_ref.dtype)
        return carry

    lax.fori_loop(0, SEQ // BLK, merge, 0, unroll=4)


def _attn_prompt(q, kv0, kv1, kv2):
    n = q.shape[0]
    pairs = GROUP_W // PAIR_W
    blk = (SEQ, PAIR_W)
    q_specs = [pl.BlockSpec(blk, functools.partial(lambda b, hp, g: (b, g * pairs + hp), g=g))
               for g in range(3)]
    k_spec = pl.BlockSpec(blk, lambda b, hp: (b, hp))
    v_spec = pl.BlockSpec(blk, lambda b, hp: (b, pairs + hp))
    scratch = [pltpu.VMEM(blk, F32) for _ in range(6)] + [pltpu.VMEM((2 * BLK, BLK), BF16)] * 2
    return pl.pallas_call(
        _attn_kernel,
        out_shape=jax.ShapeDtypeStruct((n, GROUP_W), BF16),
        grid=(n // SEQ, pairs),
        in_specs=q_specs + [k_spec, v_spec, k_spec, v_spec, k_spec, v_spec],
        out_specs=pl.BlockSpec(blk, lambda b, hp: (b, hp)),
        scratch_shapes=scratch,
        compiler_params=_params(2),
        name="attn_prompt",
    )(q, q, q, kv0, kv0, kv1, kv1, kv2, kv2)


NEW_PAD = 128
QROWS = HEADS * DEC_SEQ


def _attn_sample_kernel(q_ref, n0_ref, n1_ref, n2_ref, c0_ref, c1_ref, c2_ref, o_ref):
    new_refs = (n0_ref, n1_ref, n2_ref)
    cache_refs = (c0_ref, c1_ref, c2_ref)
    rowq = lax.broadcasted_iota(jnp.int32, (QROWS, GROUP_W), 0)
    laneq = lax.broadcasted_iota(jnp.int32, (QROWS, GROUP_W), 1)
    own_head = (rowq >> _log2(DEC_SEQ)) == (laneq >> _log2(HEAD_DIM))
    outs, lses = [], []
    for g, (window, dil) in enumerate(A_GROUPS):
        q = q_ref[:, g * GROUP_W:(g + 1) * GROUP_W]
        qbd = jnp.where(own_head, jnp.concatenate([q] * HEADS, axis=0), 0.0).astype(BF16)
        k_old = cache_refs[g][0].astype(BF16)
        v_old = cache_refs[g][1].astype(BF16)
        new = jnp.concatenate(
            [new_refs[g][...], jnp.zeros((NEW_PAD - DEC_SEQ, 2 * GROUP_W), F32)], axis=0)
        k_new = new[:, :GROUP_W].astype(BF16)
        v_new = new[:, GROUP_W:].astype(BF16)

        key = lax.broadcasted_iota(jnp.int32, (QROWS, window), 1)
        j = lax.broadcasted_iota(jnp.int32, (QROWS, window), 0) & (DEC_SEQ - 1)
        back = window + j - key
        valid_old = ((back & (dil - 1)) == 0) & (key >= j)
        jn = lax.broadcasted_iota(jnp.int32, (QROWS, NEW_PAD), 1)
        jq = lax.broadcasted_iota(jnp.int32, (QROWS, NEW_PAD), 0) & (DEC_SEQ - 1)
        valid_new = (jn <= jq) & (((jq - jn) & (dil - 1)) == 0)

        s_old = jnp.where(valid_old, _dot(qbd, k_old), NEG)
        s_new = jnp.where(valid_new, _dot_nt(qbd, k_new), NEG)
        m = jnp.maximum(jnp.max(s_old, axis=1, keepdims=True),
                        jnp.max(s_new, axis=1, keepdims=True))
        p_old = jnp.exp2(s_old - m)
        p_new = jnp.exp2(s_new - m)
        l = jnp.sum(p_old, axis=1, keepdims=True) + jnp.sum(p_new, axis=1, keepdims=True)
        o = (_dot_nt(p_old.astype(BF16), v_old) + _dot(p_new.astype(BF16), v_new)) / l
        lse = m * LN_2 + jnp.log(l)
        om = jnp.where(own_head, o, 0.0)
        lm = jnp.where(own_head, lse, 0.0)
        og, lg = om[0:DEC_SEQ], lm[0:DEC_SEQ]
        for h in range(1, HEADS):
            og = og + om[h * DEC_SEQ:(h + 1) * DEC_SEQ]
            lg = lg + lm[h * DEC_SEQ:(h + 1) * DEC_SEQ]
        outs.append(og)
        lses.append(lg)
    mx = jnp.maximum(jnp.maximum(lses[0], lses[1]), lses[2])
    e = [jnp.exp(x - mx) for x in lses]
    num = e[0] * outs[0] + e[1] * outs[1] + e[2] * outs[2]
    o_ref[...] = num / (e[0] + e[1] + e[2])


def _attn_sample(q, new0, new1, new2, c0, c1, c2):
    row_spec = lambda w: pl.BlockSpec((DEC_SEQ, w), lambda b: (b, 0))
    cache_spec = lambda c: pl.BlockSpec((None,) + c.shape[1:], lambda b: (b, 0, 0, 0))
    return pl.pallas_call(
        _attn_sample_kernel,
        out_shape=jax.ShapeDtypeStruct((DEC_BATCH * DEC_SEQ, GROUP_W), F32),
        grid=(DEC_BATCH,),
        in_specs=[
            row_spec(QKV_W), row_spec(2 * GROUP_W), row_spec(2 * GROUP_W), row_spec(2 * GROUP_W),
            cache_spec(c0), cache_spec(c1), cache_spec(c2),
        ],
        out_specs=row_spec(GROUP_W),
        compiler_params=_params(1),
        name="attn_sample",
    )(q, new0, new1, new2, c0, c1, c2)


UV_CHUNK = 512
B_GROUP_DIM = B_WIDTH // B_GROUPS


def _mixb_kernel(x_ref, g_ref, wuv_ref, vgain_ref, wsp_ref, bias_ref, wo_ref, *rest,
                 chunk, emit_v):
    if emit_v:
        o_ref, vout_ref, v_ref, h_ref = rest
    else:
        o_ref, v_ref, h_ref = rest
    tm = x_ref.shape[0]
    r = wsp_ref.shape[1]
    nsub = tm // r
    x = x_ref[...]
    xn = _rms(x, g_ref[...]).astype(BF16)
    ssq = jnp.zeros((tm, B_GROUP_DIM), F32)
    for c in range(B_WIDTH // UV_CHUNK):
        lo = c * UV_CHUNK
        v_raw = _dot(xn, wuv_ref[:, B_WIDTH + lo:B_WIDTH + lo + UV_CHUNK])
        for o in range(0, UV_CHUNK, B_GROUP_DIM):
            vv = _gelu_tanh(v_raw[:, o:o + B_GROUP_DIM])
            v_ref[:, lo + o:lo + o + B_GROUP_DIM] = vv
            ssq = ssq + vv * vv
    inv = lax.rsqrt(jnp.sum(ssq, axis=-1, keepdims=True) * (1.0 / B_WIDTH) + EPS)
    row = lax.broadcasted_iota(jnp.int32, (r, r), 0)
    col = lax.broadcasted_iota(jnp.int32, (r, r), 1)
    shift = _log2(chunk)
    keep = ((row >> shift) == (col >> shift)) & ((col & (chunk - 1)) <= (row & (chunk - 1)))
    groups_per_chunk = UV_CHUNK // B_GROUP_DIM
    stack = max(1, V7X_MXU_DEPTH // r)
    for c in range(B_WIDTH // UV_CHUNK):
        u_raw = _dot(xn, wuv_ref[:, c * UV_CHUNK:(c + 1) * UV_CHUNK])
        for gi0 in range(0, groups_per_chunk, stack):
            gis = range(gi0, gi0 + stack)
            vbs, ws = [], []
            for gi in gis:
                g = c * groups_per_chunk + gi
                lanes = slice(g * B_GROUP_DIM, (g + 1) * B_GROUP_DIM)
                vg = v_ref[:, lanes] * inv * vgain_ref[:, lanes]
                if emit_v:
                    vout_ref[:, lanes] = vg
                vb = vg.astype(BF16)
                if nsub > 1:
                    vb = jnp.concatenate([vb[s * r:(s + 1) * r] for s in range(nsub)], axis=1)
                vbs.append(vb)
                ws.append(jnp.where(keep, wsp_ref[g], 0.0).astype(BF16))
            if stack == 1:
                mixed_all = _dot(ws[0], vbs[0])
            else:
                zero = jnp.zeros((r, r), BF16)
                w_bd = jnp.concatenate(
                    [jnp.concatenate([ws[i] if i == j else zero for j in range(stack)], axis=1)
                     for i in range(stack)], axis=0)
                mixed_all = _dot(w_bd, jnp.concatenate(vbs, axis=0))
            for i, gi in enumerate(gis):
                g = c * groups_per_chunk + gi
                lanes = slice(g * B_GROUP_DIM, (g + 1) * B_GROUP_DIM)
                mixed = mixed_all[i * r:(i + 1) * r]
                bias = bias_ref[g]
                ug = _gelu_tanh(u_raw[:, gi * B_GROUP_DIM:(gi + 1) * B_GROUP_DIM])
                for s in range(nsub):
                    rows = slice(s * r, (s + 1) * r)
                    h_ref[rows, lanes] = (
                        ug[rows] * (mixed[:, s * B_GROUP_DIM:(s + 1) * B_GROUP_DIM] + bias)
                    ).astype(BF16)
    o_ref[...] = x + _dot(h_ref[...], wo_ref[...])


def _mixb(x, g, w_uv, v_gain, w_sp, b_t, w_o, tm, chunk, emit_v):
    n = x.shape[0]
    r = w_sp.shape[1]
    x_spec = pl.BlockSpec((tm, D_MODEL), lambda i: (i, 0))
    out_shape = [jax.ShapeDtypeStruct((n, D_MODEL), F32)]
    out_specs = [x_spec]
    if emit_v:
        out_shape.append(jax.ShapeDtypeStruct((n, B_WIDTH), F32))
        out_specs.append(pl.BlockSpec((tm, B_WIDTH), lambda i: (i, 0)))
    return pl.pallas_call(
        functools.partial(_mixb_kernel, chunk=chunk, emit_v=emit_v),
        out_shape=tuple(out_shape),
        grid=(n // tm,),
        in_specs=[
            x_spec,
            _resident((1, D_MODEL)),
            _resident((D_MODEL, 2 * B_WIDTH)),
            _resident((1, B_WIDTH)),
            _resident((B_GROUPS, r, r)),
            _resident((B_GROUPS, r, B_GROUP_DIM)),
            _resident((B_WIDTH, D_MODEL)),
        ],
        out_specs=tuple(out_specs),
        scratch_shapes=[
            pltpu.VMEM((tm, B_WIDTH), F32),
            pltpu.VMEM((tm, B_WIDTH), BF16),
        ],
        compiler_params=_params(1),
        name="spatial_gating",
    )(x, g, w_uv, v_gain, w_sp, b_t, w_o)


def _rope_angles(pos):
    half = HEAD_DIM // 2
    inv = ROPE_THETA ** (-jnp.arange(half, dtype=F32) / half)
    ang = pos.astype(F32)[:, None] * inv[None, :]
    return jnp.cos(ang), jnp.sin(ang)


def _pair_order(width):
    half = HEAD_DIM // 2
    n = jnp.arange(width)
    lane = n % PAIR_W
    head, hi = (lane // half) % 2, lane // HEAD_DIM
    return (n // PAIR_W) * PAIR_W + head * HEAD_DIM + hi * half + lane % half


def _rope_tables(pos):
    cos, sin = _rope_angles(pos)
    cos_t = jnp.tile(cos, (1, QK_BLOCK // (HEAD_DIM // 2)))
    sin_t = jnp.tile(jnp.concatenate([-sin, -sin, sin, sin], axis=-1), (1, QK_BLOCK // PAIR_W))
    return cos_t, sin_t


def _feature_major(cache):
    _, b, length, two, h, hd = cache.shape
    return jnp.transpose(cache[0], (0, 2, 3, 4, 1)).reshape(b, two, h * hd, length)


def _position_major(tail):
    b, two, _, length = tail.shape
    t = tail.reshape(b, two, HEADS, HEAD_DIM, length)
    return jnp.transpose(t, (0, 4, 1, 2, 3))[None]


def kernel(x_prompt, x_sample, cache_kv_w128, cache_kv_w512, cache_kv_w2048, norm_g,
           w_ffn_in, w_ffn_out, w_qkv_a, qk_gain_a, w_o_a, w_uv_b, v_gain_b, w_sp_b, b_sp_b,
           w_o_b):
    xp = x_prompt.reshape(BATCH * SEQ, D_MODEL)
    xs = x_sample.reshape(DEC_BATCH * DEC_SEQ, D_MODEL)
    n_s = DEC_BATCH * DEC_SEQ
    tm_p = 1024

    w_in = w_ffn_in.astype(BF16)
    w_out = w_ffn_out.astype(BF16)
    order = _pair_order(QKV_W)
    w_qkv = jnp.concatenate(
        [w_qkv_a[0][:, :QKV_W][:, order], w_qkv_a[0][:, QKV_W:2 * QKV_W][:, order],
         w_qkv_a[0][:, 2 * QKV_W:]], axis=1).astype(BF16)
    w_oa = w_o_a[0].astype(BF16)
    w_uv = w_uv_b[0].astype(BF16)
    w_ob = w_o_b[0].astype(BF16)
    gn = norm_g.reshape(2, 3, 1, D_MODEL)

    natural = order[:QK_BLOCK]
    head_id = natural // HEAD_DIM
    bd = jnp.where(head_id[:, None] == head_id[None, :], 1.0 / HEAD_DIM, 0.0).astype(BF16)
    gain = qk_gain_a[0][:, natural % HEAD_DIM] * jnp.array([[Q_SCALE], [1.0]], F32)
    cos_p, sin_p = _rope_tables(jnp.arange(SEQ))
    cos_s, sin_s = _rope_tables(PAST_LEN + jnp.arange(DEC_SEQ))
    cos_s = jnp.tile(cos_s, (DEC_BATCH, 1))
    sin_s = jnp.tile(sin_s, (DEC_BATCH, 1))

    xp = _ffn(xp, gn[0, 0], w_in, w_out, 0, 0, tm_p)
    xs = _ffn(xs, gn[0, 0], w_in, w_out, 0, 0, n_s)

    qp, kvp0 = _qkv(xp, gn[0, 1], w_qkv, gain, cos_p, sin_p, bd, QKV_TM, ("q", 0))
    kvp1, kvp2 = _qkv(xp, gn[0, 1], w_qkv, gain, cos_p, sin_p, bd, QKV_TM, (1, 2))
    qs, kvs0, kvs1, kvs2 = _qkv(xs, gn[0, 1], w_qkv, gain, cos_s, sin_s, bd, n_s, ("q", 0, 1, 2))
    qs = qs[:, order]
    kvs0, kvs1, kvs2 = (
        jnp.concatenate([kv[:, :GROUP_W][:, order[:GROUP_W]], kv[:, GROUP_W:]], axis=1)
        for kv in (kvs0, kvs1, kvs2))

    groups = len(A_GROUPS)
    w_kv_t = jnp.transpose(
        w_qkv_a[0][:, QKV_W:].reshape(D_MODEL, 2, groups, GROUP_W), (2, 1, 3, 0)
    ).reshape(groups, 2 * GROUP_W, D_MODEL).astype(BF16)
    cos_a, sin_a = _rope_angles(jnp.arange(SEQ))
    tail0, tail1, tail2 = _tails(xp, gn[0, 1], w_kv_t, qk_gain_a[0, 1].reshape(HEAD_DIM, 1),
                                 cos_a.T, sin_a.T)

    mp = _attn_prompt(qp, kvp0, kvp1, kvp2)
    ms = _attn_sample(qs, kvs0, kvs1, kvs2, _feature_major(cache_kv_w128),
                      _feature_major(cache_kv_w512), _feature_major(cache_kv_w2048))

    xp = _ffn(xp, gn[0, 2], w_in, w_out, 0, 1, tm_p, proj=(mp, w_oa))
    xs = _ffn(xs, gn[0, 2], w_in, w_out, 0, 1, n_s, proj=(ms, w_oa))

    xp = _ffn(xp, gn[1, 0], w_in, w_out, 1, 0, tm_p)
    xs = _ffn(xs, gn[1, 0], w_in, w_out, 1, 0, n_s)

    vgain = v_gain_b[0].reshape(1, B_WIDTH)
    def lane_replicated(b):
        return jnp.broadcast_to(b[:, :, None], b.shape + (B_GROUP_DIM,))

    (xp,) = _mixb(xp, gn[1, 1], w_uv, vgain, w_sp_b[0], lane_replicated(b_sp_b[0]), w_ob,
                  tm=1024, chunk=B_CHUNK, emit_v=False)
    rep = (jnp.arange(n_s)[:, None] % DEC_SEQ == jnp.arange(DEC_SEQ)[None, :]).astype(F32)
    w_sp_s = jnp.einsum("it,gts,js->gij", rep, w_sp_b[0][:, :DEC_SEQ, :DEC_SEQ], rep,
                        precision=lax.Precision.HIGHEST)
    b_s = jnp.tile(b_sp_b[0][:, :DEC_SEQ], (1, DEC_BATCH))
    xs, v_new = _mixb(xs, gn[1, 1], w_uv, vgain, w_sp_s, lane_replicated(b_s), w_ob,
                      tm=n_s, chunk=DEC_SEQ, emit_v=True)

    xp = _ffn(xp, gn[1, 2], w_in, w_out, 1, 1, tm_p)
    xs = _ffn(xs, gn[1, 2], w_in, w_out, 1, 1, n_s)

    def rows(kv):
        return kv.reshape(1, DEC_BATCH, DEC_SEQ, 2, HEADS, HEAD_DIM)

    return (
        xp.reshape(BATCH, SEQ, D_MODEL),
        xs.reshape(DEC_BATCH, DEC_SEQ, D_MODEL),
        _position_major(tail0), _position_major(tail1), _position_major(tail2),
        rows(kvs0), rows(kvs1), rows(kvs2),
        v_new.reshape(1, DEC_BATCH, DEC_SEQ, B_WIDTH),
    )
```

```python
import functools

import jax
import jax.numpy as jnp
from jax import lax
from jax.experimental import pallas as pl
from jax.experimental.pallas import tpu as pltpu

F32 = jnp.float32
BF16 = jnp.bfloat16

D_MODEL = 1024
D_FF = 2816
HEAD_DIM = 64
HEADS = 8
GROUP_W = HEADS * HEAD_DIM
PAIR_W = 2 * HEAD_DIM
QKV_W = 3 * GROUP_W
A_GROUPS = ((128, 1), (512, 4), (2048, 16))
ROPE_THETA = 10000.0
B_WIDTH = 2048
B_GROUPS = 16
B_CHUNK = 128
EPS = 1e-6
NEG = -1e30
LOG2_E = 1.4426950408889634
LN_2 = 0.6931471805599453
Q_SCALE = HEAD_DIM ** -0.5 * LOG2_E

SEQ = 2048
BATCH = 16
DEC_BATCH = 32
DEC_SEQ = 8
PAST_LEN = 16384

V7X_VMEM_LIMIT_BYTES = 56 * 1024 * 1024
V7X_MXU_DEPTH = 256


def _log2(n):
    assert n > 0 and n & (n - 1) == 0, n
    return n.bit_length() - 1


def _dot(a, b):
    return jnp.dot(a, b, preferred_element_type=F32)


def _dot_nt(a, b):
    return lax.dot_general(a, b, (((1,), (1,)), ((), ())), preferred_element_type=F32)


def _rms(x, g):
    ms = jnp.mean(x * x, axis=-1, keepdims=True)
    return x * lax.rsqrt(ms + EPS) * g


def _rms_operand(x, g):
    r = lax.rsqrt(jnp.mean(x * x, axis=-1, keepdims=True) + EPS)
    return (x * g).astype(BF16), r


def _gelu_tanh(x):
    cdf = 0.5 * (1.0 + jnp.tanh(0.7978845608028654 * (x + 0.044715 * (x * x * x))))
    return x * cdf


def _params(n_axes):
    return pltpu.CompilerParams(
        dimension_semantics=("arbitrary",) * n_axes,
        vmem_limit_bytes=V7X_VMEM_LIMIT_BYTES,
    )


def _resident(shape, index=None):
    if index is None:
        index = (0,) * len(shape)
    return pl.BlockSpec(shape, lambda *_: index, pipeline_mode=pl.Buffered(1))


FF_CHUNK = 256


def _ffn_kernel(*refs, with_proj):
    if with_proj:
        x_ref, m_ref, wo_ref, g_ref, win_ref, wout_ref, o_ref, h_ref = refs
        x = x_ref[...] + _dot(m_ref[...].astype(BF16), wo_ref[...])
    else:
        x_ref, g_ref, win_ref, wout_ref, o_ref, h_ref = refs
        x = x_ref[...]
    xb, r = _rms_operand(x, g_ref[...])
    for c in range(D_FF // FF_CHUNK):
        lo = c * FF_CHUNK
        gate = _dot(xb, win_ref[:, lo:lo + FF_CHUNK]) * r
        up = _dot(xb, win_ref[:, D_FF + lo:D_FF + lo + FF_CHUNK]) * r
        act = gate * (1.0 / (1.0 + jnp.exp(-gate)))
        h_ref[:, lo:lo + FF_CHUNK] = (act * up).astype(BF16)
    o_ref[...] = x + 0.5 * _dot(h_ref[...], wout_ref[...])


def _ffn(x, g, w_in, w_out, layer, step, tm, proj=None):
    n = x.shape[0]
    x_spec = pl.BlockSpec((tm, D_MODEL), lambda i: (i, 0))
    in_specs = [x_spec]
    args = [x]
    if proj is not None:
        m, w_o = proj
        in_specs += [pl.BlockSpec((tm, GROUP_W), lambda i: (i, 0)), _resident((GROUP_W, D_MODEL))]
        args += [m, w_o]
    in_specs += [
        _resident((1, D_MODEL)),
        _resident((None, None, D_MODEL, 2 * D_FF), (layer, step, 0, 0)),
        _resident((None, None, D_FF, D_MODEL), (layer, step, 0, 0)),
    ]
    args += [g, w_in, w_out]
    return pl.pallas_call(
        functools.partial(_ffn_kernel, with_proj=proj is not None),
        out_shape=jax.ShapeDtypeStruct((n, D_MODEL), F32),
        grid=(n // tm,),
        in_specs=in_specs,
        out_specs=x_spec,
        scratch_shapes=[pltpu.VMEM((tm, D_FF), BF16)],
        compiler_params=_params(1),
        name="ffn",
    )(*args)


QK_BLOCK = 256
QKV_TM = 1024


def _qkv_kernel(x_ref, g_ref, w_ref, gain_ref, cos_ref, sin_ref, bd_ref, *out_refs, parts):
    xn = _rms(x_ref[...], g_ref[...]).astype(BF16)
    cos = cos_ref[...]
    sin = sin_ref[...]
    bd = bd_ref[...]

    def norm_rot(y, gain):
        ms = _dot((y * y).astype(BF16), bd)
        yn = y * lax.rsqrt(ms + EPS) * gain
        partner = jnp.concatenate(
            [pltpu.roll(yn[:, lo:lo + PAIR_W], PAIR_W // 2, 1)
             for lo in range(0, QK_BLOCK, PAIR_W)], axis=1)
        return yn * cos + partner * sin

    for part, o_ref in zip(parts, out_refs):
        if part == "q":
            for c0 in range(0, QKV_W, GROUP_W):
                y = _dot(xn, w_ref[:, c0:c0 + GROUP_W])
                for o0 in range(0, GROUP_W, QK_BLOCK):
                    o_ref[:, c0 + o0:c0 + o0 + QK_BLOCK] = norm_rot(
                        y[:, o0:o0 + QK_BLOCK], gain_ref[0:1, :])
            continue
        c0 = part * GROUP_W
        yk = _dot(xn, w_ref[:, QKV_W + c0:QKV_W + c0 + GROUP_W])
        for o0 in range(0, GROUP_W, QK_BLOCK):
            o_ref[:, o0:o0 + QK_BLOCK] = norm_rot(yk[:, o0:o0 + QK_BLOCK], gain_ref[1:2, :])
        o_ref[:, GROUP_W:] = _dot(xn, w_ref[:, 2 * QKV_W + c0:2 * QKV_W + c0 + GROUP_W])


def _qkv(x, g, w, gain, cos, sin, bd, tm, parts):
    n = x.shape[0]
    pos_blocks = cos.shape[0] // tm
    widths = [QKV_W if part == "q" else 2 * GROUP_W for part in parts]
    return pl.pallas_call(
        functools.partial(_qkv_kernel, parts=parts),
        out_shape=tuple(jax.ShapeDtypeStruct((n, wd), F32) for wd in widths),
        grid=(n // tm,),
        in_specs=[
            pl.BlockSpec((tm, D_MODEL), lambda i: (i, 0)),
            _resident((1, D_MODEL)),
            _resident((D_MODEL, 3 * QKV_W)),
            _resident((2, QK_BLOCK)),
            pl.BlockSpec((tm, QK_BLOCK), lambda i: (i % pos_blocks, 0)),
            pl.BlockSpec((tm, QK_BLOCK), lambda i: (i % pos_blocks, 0)),
            _resident((QK_BLOCK, QK_BLOCK)),
        ],
        out_specs=tuple(pl.BlockSpec((tm, wd), lambda i: (i, 0)) for wd in widths),
        compiler_params=_params(1),
        name="qkv",
    )(x, g, w, gain, cos, sin, bd)


TAIL_TM = 1024


def _tails_kernel(x_ref, g_ref, wt_ref, gain_ref, cos_ref, sin_ref, t0_ref, t1_ref, t2_ref,
                  *, tiles_per_seq):
    tm = x_ref.shape[0]
    half = HEAD_DIM // 2
    xn = _rms(x_ref[...], g_ref[...]).astype(BF16)
    gain = gain_ref[...]
    is_last_tile = (pl.program_id(0) % tiles_per_seq) == tiles_per_seq - 1

    def put_tail(g, t_ref):
        keep = t_ref.shape[-1]
        lo = tm - keep
        kv = _dot_nt(wt_ref[g], xn[lo:, :])
        cos = cos_ref[:, lo:]
        sin = sin_ref[:, lo:]
        for h in range(HEADS):
            y = kv[h * HEAD_DIM:(h + 1) * HEAD_DIM, :]
            ms = jnp.mean(y * y, axis=0, keepdims=True)
            yn = y * lax.rsqrt(ms + EPS) * gain
            x1, x2 = yn[:half], yn[half:]
            t_ref[0, h * HEAD_DIM:h * HEAD_DIM + half, :] = x1 * cos - x2 * sin
            t_ref[0, h * HEAD_DIM + half:(h + 1) * HEAD_DIM, :] = x2 * cos + x1 * sin
        t_ref[1] = kv[GROUP_W:]

    for g, t_ref in enumerate((t0_ref, t1_ref, t2_ref)):
        if A_GROUPS[g][0] >= SEQ:
            put_tail(g, t_ref)
        else:
            pl.when(is_last_tile)(functools.partial(put_tail, g, t_ref))


def _tails(x, g, wt, gain_col, cos_t, sin_t):
    n = x.shape[0]
    tm = TAIL_TM
    pos_blocks = SEQ // tm
    out_shape, out_specs = [], []
    for window, _ in A_GROUPS:
        keep = min(window, SEQ)
        assert keep <= tm or keep == SEQ
        out_shape.append(jax.ShapeDtypeStruct((n // SEQ, 2, GROUP_W, keep), F32))
        if keep > tm:
            index = lambda i: (i // pos_blocks, 0, 0, i % pos_blocks)
        else:
            index = lambda i: (i // pos_blocks, 0, 0, 0)
        out_specs.append(pl.BlockSpec((None, 2, GROUP_W, min(keep, tm)), index))
    return pl.pallas_call(
        functools.partial(_tails_kernel, tiles_per_seq=pos_blocks),
        out_shape=tuple(out_shape),
        grid=(n // tm,),
        in_specs=[
            pl.BlockSpec((tm, D_MODEL), lambda i: (i, 0)),
            _resident((1, D_MODEL)),
            _resident((len(A_GROUPS), 2 * GROUP_W, D_MODEL)),
            _resident((HEAD_DIM, 1)),
            pl.BlockSpec((HEAD_DIM // 2, tm), lambda i: (0, i % pos_blocks)),
            pl.BlockSpec((HEAD_DIM // 2, tm), lambda i: (0, i % pos_blocks)),
        ],
        out_specs=tuple(out_specs),
        compiler_params=_params(1),
        name="kv_tails",
    )(x, g, wt, gain_col, cos_t, sin_t)


BLK = 128
SPAN = 4


def _attn_kernel(q0_ref, q1_ref, q2_ref, k0_ref, v0_ref, k1_ref, v1_ref, k2_ref, v2_ref,
                 o_ref, og0, og1, og2, lg0, lg1, lg2, kmask_ref, qsel_ref):
    lane = lax.broadcasted_iota(jnp.int32, (BLK, PAIR_W), 1)
    head_a = lane < HEAD_DIM
    lane_ov = lax.broadcasted_iota(jnp.int32, (BLK, 2 * PAIR_W), 1)
    head_a2 = (lane_ov & (PAIR_W - 1)) < HEAD_DIM
    row2 = lax.broadcasted_iota(jnp.int32, (2 * BLK, PAIR_W), 0)
    lane2 = lax.broadcasted_iota(jnp.int32, (2 * BLK, PAIR_W), 1)
    q_keep = (row2 >> _log2(BLK)) == ((lane2 >> _log2(HEAD_DIM // 2)) & 1)
    key = lax.broadcasted_iota(jnp.int32, (2 * BLK, BLK), 0)
    qi = lax.broadcasted_iota(jnp.int32, (2 * BLK, BLK), 1)
    jk = key & (BLK - 1)
    valid = ((key < BLK) & (jk >= qi)) | ((key >= BLK) & (jk <= qi))
    kmask_ref[...] = jnp.where(valid, 0.0, NEG).astype(BF16)
    qsel_ref[...] = jnp.where(jk == qi, 1.0, 0.0).astype(BF16)

    def one_block(q, kb, vb, kmask):
        q2 = jnp.where(q_keep, jnp.concatenate([q, q], axis=0), 0.0).astype(BF16)
        s = _dot_nt(jnp.concatenate([q2, qsel_ref[...]], axis=1),
                    jnp.concatenate([kb, kmask], axis=1))
        m = jnp.max(s, axis=1, keepdims=True)
        p = jnp.exp2(s - m).astype(BF16)
        ones = jnp.ones((vb.shape[0], PAIR_W), BF16)
        pv = _dot(p, jnp.concatenate([vb, ones], axis=1))
        pv = jnp.where(head_a2, pv[:BLK], pv[BLK:])
        m = jnp.where(head_a, m[:BLK], m[BLK:])
        l = pv[:, PAIR_W:]
        return pv[:, :PAIR_W] / l, m * LN_2 + jnp.log(l)

    def span(refs, dil, r, n0, count, has_prev):
        q_ref, k_ref, v_ref, og_ref, lg_ref = refs

        def rows(n):
            return pl.ds(r + (n0 + n) * (dil * BLK), BLK, stride=dil)

        first = -1 if has_prev else 0
        kb = {n: k_ref[rows(n), :].astype(BF16) for n in range(first, count)}
        vb = {n: v_ref[rows(n), :].astype(BF16) for n in range(first, count)}
        for n in range(count):
            q = q_ref[rows(n), :]
            if n - 1 in kb:
                kcat = jnp.concatenate([kb[n - 1], kb[n]], axis=0)
                vcat = jnp.concatenate([vb[n - 1], vb[n]], axis=0)
                o, lse = one_block(q, kcat, vcat, kmask_ref[...])
            else:
                o, lse = one_block(q, kb[n], vb[n], kmask_ref[BLK:, :])
            og_ref[rows(n), :] = o
            lg_ref[rows(n), :] = lse

    refs0 = (q0_ref, k0_ref, v0_ref, og0, lg0)
    refs1 = (q1_ref, k1_ref, v1_ref, og1, lg1)
    refs2 = (q2_ref, k2_ref, v2_ref, og2, lg2)
    dil0, dil1, dil2 = (d for _, d in A_GROUPS)
    blocks0, blocks1, blocks2 = (SEQ // (d * BLK) for _, d in A_GROUPS)
    assert blocks0 % SPAN == 0 and blocks1 == SPAN and blocks2 == 1 and dil2 % SPAN == 0

    for it in range(blocks0 // SPAN):
        span(refs0, dil0, 0, it * SPAN, SPAN, it > 0)
    for r in range(dil1):
        span(refs1, dil1, r, 0, SPAN, False)
    for r in range(dil2):
        span(refs2, dil2, r, 0, 1, False)

    def merge(i, carry):
        rows = pl.ds(pl.multiple_of(i * BLK, BLK), BLK)
        l0, l1, l2 = lg0[rows, :], lg1[rows, :], lg2[rows, :]
        mx = jnp.maximum(jnp.maximum(l0, l1), l2)
        e0, e1, e2 = jnp.exp(l0 - mx), jnp.exp(l1 - mx), jnp.exp(l2 - mx)
        num = e0 * og0[rows, :] + e1 * og1[rows, :] + e2 * og2[rows, :]
        o_ref[rows, :] = (num / (e0 + e1 + e2)).astype(o_ref.dtype)
        return carry

    lax.fori_loop(0, SEQ // BLK, merge, 0, unroll=4)


def _attn_prompt(q, kv0, kv1, kv2):
    n = q.shape[0]
    pairs = GROUP_W // PAIR_W
    blk = (SEQ, PAIR_W)
    q_specs = [pl.BlockSpec(blk, functools.partial(lambda b, hp, g: (b, g * pairs + hp), g=g))
               for g in range(3)]
    k_spec = pl.BlockSpec(blk, lambda b, hp: (b, hp))
    v_spec = pl.BlockSpec(blk, lambda b, hp: (b, pairs + hp))
    scratch = [pltpu.VMEM(blk, F32) for _ in range(6)] + [pltpu.VMEM((2 * BLK, BLK), BF16)] * 2
    return pl.pallas_call(
        _attn_kernel,
        out_shape=jax.ShapeDtypeStruct((n, GROUP_W), BF16),
        grid=(n // SEQ, pairs),
        in_specs=q_specs + [k_spec, v_spec, k_spec, v_spec, k_spec, v_spec],
        out_specs=pl.BlockSpec(blk, lambda b, hp: (b, hp)),
        scratch_shapes=scratch,
        compiler_params=_params(2),
        name="attn_prompt",
    )(q, q, q, kv0, kv0, kv1, kv1, kv2, kv2)


NEW_PAD = 128
QROWS = HEADS * DEC_SEQ


def _attn_sample_kernel(q_ref, n0_ref, n1_ref, n2_ref, c0_ref, c1_ref, c2_ref, o_ref):
    new_refs = (n0_ref, n1_ref, n2_ref)
    cache_refs = (c0_ref, c1_ref, c2_ref)
    rowq = lax.broadcasted_iota(jnp.int32, (QROWS, GROUP_W), 0)
    laneq = lax.broadcasted_iota(jnp.int32, (QROWS, GROUP_W), 1)
    own_head = (rowq >> _log2(DEC_SEQ)) == (laneq >> _log2(HEAD_DIM))
    outs, lses = [], []
    for g, (window, dil) in enumerate(A_GROUPS):
        q = q_ref[:, g * GROUP_W:(g + 1) * GROUP_W]
        qbd = jnp.where(own_head, jnp.concatenate([q] * HEADS, axis=0), 0.0).astype(BF16)
        k_old = cache_refs[g][0].astype(BF16)
        v_old = cache_refs[g][1].astype(BF16)
        new = jnp.concatenate(
            [new_refs[g][...], jnp.zeros((NEW_PAD - DEC_SEQ, 2 * GROUP_W), F32)], axis=0)
        k_new = new[:, :GROUP_W].astype(BF16)
        v_new = new[:, GROUP_W:].astype(BF16)

        key = lax.broadcasted_iota(jnp.int32, (QROWS, window), 1)
        j = lax.broadcasted_iota(jnp.int32, (QROWS, window), 0) & (DEC_SEQ - 1)
        back = window + j - key
        valid_old = ((back & (dil - 1)) == 0) & (key >= j)
        jn = lax.broadcasted_iota(jnp.int32, (QROWS, NEW_PAD), 1)
        jq = lax.broadcasted_iota(jnp.int32, (QROWS, NEW_PAD), 0) & (DEC_SEQ - 1)
        valid_new = (jn <= jq) & (((jq - jn) & (dil - 1)) == 0)

        s_old = jnp.where(valid_old, _dot(qbd, k_old), NEG)
        s_new = jnp.where(valid_new, _dot_nt(qbd, k_new), NEG)
        m = jnp.maximum(jnp.max(s_old, axis=1, keepdims=True),
                        jnp.max(s_new, axis=1, keepdims=True))
        p_old = jnp.exp2(s_old - m)
        p_new = jnp.exp2(s_new - m)
        l = jnp.sum(p_old, axis=1, keepdims=True) + jnp.sum(p_new, axis=1, keepdims=True)
        o = (_dot_nt(p_old.astype(BF16), v_old) + _dot(p_new.astype(BF16), v_new)) / l
        lse = m * LN_2 + jnp.log(l)
        om = jnp.where(own_head, o, 0.0)
        lm = jnp.where(own_head, lse, 0.0)
        og, lg = om[0:DEC_SEQ], lm[0:DEC_SEQ]
        for h in range(1, HEADS):
            og = og + om[h * DEC_SEQ:(h + 1) * DEC_SEQ]
            lg = lg + lm[h * DEC_SEQ:(h + 1) * DEC_SEQ]
        outs.append(og)
        lses.append(lg)
    mx = jnp.maximum(jnp.maximum(lses[0], lses[1]), lses[2])
    e = [jnp.exp(x - mx) for x in lses]
    num = e[0] * outs[0] + e[1] * outs[1] + e[2] * outs[2]
    o_ref[...] = num / (e[0] + e[1] + e[2])


def _attn_sample(q, new0, new1, new2, c0, c1, c2):
    row_spec = lambda w: pl.BlockSpec((DEC_SEQ, w), lambda b: (b, 0))
    cache_spec = lambda c: pl.BlockSpec((None,) + c.shape[1:], lambda b: (b, 0, 0, 0))
    return pl.pallas_call(
        _attn_sample_kernel,
        out_shape=jax.ShapeDtypeStruct((DEC_BATCH * DEC_SEQ, GROUP_W), F32),
        grid=(DEC_BATCH,),
        in_specs=[
            row_spec(QKV_W), row_spec(2 * GROUP_W), row_spec(2 * GROUP_W), row_spec(2 * GROUP_W),
            cache_spec(c0), cache_spec(c1), cache_spec(c2),
        ],
        out_specs=row_spec(GROUP_W),
        compiler_params=_params(1),
        name="attn_sample",
    )(q, new0, new1, new2, c0, c1, c2)


UV_CHUNK = 512
B_GROUP_DIM = B_WIDTH // B_GROUPS


def _mixb_kernel(x_ref, g_ref, wuv_ref, vgain_ref, wsp_ref, bias_ref, wo_ref, *rest,
                 chunk, emit_v):
    if emit_v:
        o_ref, vout_ref, v_ref, h_ref = rest
    else:
        o_ref, v_ref, h_ref = rest
    tm = x_ref.shape[0]
    r = wsp_ref.shape[1]
    nsub = tm // r
    x = x_ref[...]
    xn = _rms(x, g_ref[...]).astype(BF16)
    ssq = jnp.zeros((tm, B_GROUP_DIM), F32)
    for c in range(B_WIDTH // UV_CHUNK):
        lo = c * UV_CHUNK
        v_raw = _dot(xn, wuv_ref[:, B_WIDTH + lo:B_WIDTH + lo + UV_CHUNK])
        for o in range(0, UV_CHUNK, B_GROUP_DIM):
            vv = _gelu_tanh(v_raw[:, o:o + B_GROUP_DIM])
            v_ref[:, lo + o:lo + o + B_GROUP_DIM] = vv
            ssq = ssq + vv * vv
    inv = lax.rsqrt(jnp.sum(ssq, axis=-1, keepdims=True) * (1.0 / B_WIDTH) + EPS)
    row = lax.broadcasted_iota(jnp.int32, (r, r), 0)
    col = lax.broadcasted_iota(jnp.int32, (r, r), 1)
    shift = _log2(chunk)
    keep = ((row >> shift) == (col >> shift)) & ((col & (chunk - 1)) <= (row & (chunk - 1)))
    groups_per_chunk = UV_CHUNK // B_GROUP_DIM
    stack = max(1, V7X_MXU_DEPTH // r)
    for c in range(B_WIDTH // UV_CHUNK):
        u_raw = _dot(xn, wuv_ref[:, c * UV_CHUNK:(c + 1) * UV_CHUNK])
        for gi0 in range(0, groups_per_chunk, stack):
            gis = range(gi0, gi0 + stack)
            vbs, ws = [], []
            for gi in gis:
                g = c * groups_per_chunk + gi
                lanes = slice(g * B_GROUP_DIM, (g + 1) * B_GROUP_DIM)
                vg = v_ref[:, lanes] * inv * vgain_ref[:, lanes]
                if emit_v:
                    vout_ref[:, lanes] = vg
                vb = vg.astype(BF16)
                if nsub > 1:
                    vb = jnp.concatenate([vb[s * r:(s + 1) * r] for s in range(nsub)], axis=1)
                vbs.append(vb)
                ws.append(jnp.where(keep, wsp_ref[g], 0.0).astype(BF16))
            if stack == 1:
                mixed_all = _dot(ws[0], vbs[0])
            else:
                zero = jnp.zeros((r, r), BF16)
                w_bd = jnp.concatenate(
                    [jnp.concatenate([ws[i] if i == j else zero for j in range(stack)], axis=1)
                     for i in range(stack)], axis=0)
                mixed_all = _dot(w_bd, jnp.concatenate(vbs, axis=0))
            for i, gi in enumerate(gis):
                g = c * groups_per_chunk + gi
                lanes = slice(g * B_GROUP_DIM, (g + 1) * B_GROUP_DIM)
                mixed = mixed_all[i * r:(i + 1) * r]
                bias = bias_ref[g]
                ug = _gelu_tanh(u_raw[:, gi * B_GROUP_DIM:(gi + 1) * B_GROUP_DIM])
                for s in range(nsub):
                    rows = slice(s * r, (s + 1) * r)
                    h_ref[rows, lanes] = (
                        ug[rows] * (mixed[:, s * B_GROUP_DIM:(s + 1) * B_GROUP_DIM] + bias)
                    ).astype(BF16)
    o_ref[...] = x + _dot(h_ref[...], wo_ref[...])


def _mixb(x, g, w_uv, v_gain, w_sp, b_t, w_o, tm, chunk, emit_v):
    n = x.shape[0]
    r = w_sp.shape[1]
    x_spec = pl.BlockSpec((tm, D_MODEL), lambda i: (i, 0))
    out_shape = [jax.ShapeDtypeStruct((n, D_MODEL), F32)]
    out_specs = [x_spec]
    if emit_v:
        out_shape.append(jax.ShapeDtypeStruct((n, B_WIDTH), F32))
        out_specs.append(pl.BlockSpec((tm, B_WIDTH), lambda i: (i, 0)))
    return pl.pallas_call(
        functools.partial(_mixb_kernel, chunk=chunk, emit_v=emit_v),
        out_shape=tuple(out_shape),
        grid=(n // tm,),
        in_specs=[
            x_spec,
            _resident((1, D_MODEL)),
            _resident((D_MODEL, 2 * B_WIDTH)),
            _resident((1, B_WIDTH)),
            _resident((B_GROUPS, r, r)),
            _resident((B_GROUPS, r, B_GROUP_DIM)),
            _resident((B_WIDTH, D_MODEL)),
        ],
        out_specs=tuple(out_specs),
        scratch_shapes=[
            pltpu.VMEM((tm, B_WIDTH), F32),
            pltpu.VMEM((tm, B_WIDTH), BF16),
        ],
        compiler_params=_params(1),
        name="spatial_gating",
    )(x, g, w_uv, v_gain, w_sp, b_t, w_o)


def _rope_angles(pos):
    half = HEAD_DIM // 2
    inv = ROPE_THETA ** (-jnp.arange(half, dtype=F32) / half)
    ang = pos.astype(F32)[:, None] * inv[None, :]
    return jnp.cos(ang), jnp.sin(ang)


def _pair_order(width):
    half = HEAD_DIM // 2
    n = jnp.arange(width)
    lane = n % PAIR_W
    head, hi = (lane // half) % 2, lane // HEAD_DIM
    return (n // PAIR_W) * PAIR_W + head * HEAD_DIM + hi * half + lane % half


def _rope_tables(pos):
    cos, sin = _rope_angles(pos)
    cos_t = jnp.tile(cos, (1, QK_BLOCK // (HEAD_DIM // 2)))
    sin_t = jnp.tile(jnp.concatenate([-sin, -sin, sin, sin], axis=-1), (1, QK_BLOCK // PAIR_W))
    return cos_t, sin_t


def _feature_major(cache):
    _, b, length, two, h, hd = cache.shape
    return jnp.transpose(cache[0], (0, 2, 3, 4, 1)).reshape(b, two, h * hd, length)


def _position_major(tail):
    b, two, _, length = tail.shape
    t = tail.reshape(b, two, HEADS, HEAD_DIM, length)
    return jnp.transpose(t, (0, 4, 1, 2, 3))[None]


def kernel(x_prompt, x_sample, cache_kv_w128, cache_kv_w512, cache_kv_w2048, norm_g,
           w_ffn_in, w_ffn_out, w_qkv_a, qk_gain_a, w_o_a, w_uv_b, v_gain_b, w_sp_b, b_sp_b,
           w_o_b):
    xp = x_prompt.reshape(BATCH * SEQ, D_MODEL)
    xs = x_sample.reshape(DEC_BATCH * DEC_SEQ, D_MODEL)
    n_s = DEC_BATCH * DEC_SEQ
    tm_p = 1024

    w_in = w_ffn_in.astype(BF16)
    w_out = w_ffn_out.astype(BF16)
    order = _pair_order(QKV_W)
    w_qkv = jnp.concatenate(
        [w_qkv_a[0][:, :QKV_W][:, order], w_qkv_a[0][:, QKV_W:2 * QKV_W][:, order],
         w_qkv_a[0][:, 2 * QKV_W:]], axis=1).astype(BF16)
    w_oa = w_o_a[0].astype(BF16)
    w_uv = w_uv_b[0].astype(BF16)
    w_ob = w_o_b[0].astype(BF16)
    gn = norm_g.reshape(2, 3, 1, D_MODEL)

    natural = order[:QK_BLOCK]
    head_id = natural // HEAD_DIM
    bd = jnp.where(head_id[:, None] == head_id[None, :], 1.0 / HEAD_DIM, 0.0).astype(BF16)
    gain = qk_gain_a[0][:, natural % HEAD_DIM] * jnp.array([[Q_SCALE], [1.0]], F32)
    cos_p, sin_p = _rope_tables(jnp.arange(SEQ))
    cos_s, sin_s = _rope_tables(PAST_LEN + jnp.arange(DEC_SEQ))
    cos_s = jnp.tile(cos_s, (DEC_BATCH, 1))
    sin_s = jnp.tile(sin_s, (DEC_BATCH, 1))

    xp = _ffn(xp, gn[0, 0], w_in, w_out, 0, 0, tm_p)
    xs = _ffn(xs, gn[0, 0], w_in, w_out, 0, 0, n_s)

    qp, kvp0 = _qkv(xp, gn[0, 1], w_qkv, gain, cos_p, sin_p, bd, QKV_TM, ("q", 0))
    kvp1, kvp2 = _qkv(xp, gn[0, 1], w_qkv, gain, cos_p, sin_p, bd, QKV_TM, (1, 2))
    qs, kvs0, kvs1, kvs2 = _qkv(xs, gn[0, 1], w_qkv, gain, cos_s, sin_s, bd, n_s, ("q", 0, 1, 2))
    qs = qs[:, order]
    kvs0, kvs1, kvs2 = (
        jnp.concatenate([kv[:, :GROUP_W][:, order[:GROUP_W]], kv[:, GROUP_W:]], axis=1)
        for kv in (kvs0, kvs1, kvs2))

    groups = len(A_GROUPS)
    w_kv_t = jnp.transpose(
        w_qkv_a[0][:, QKV_W:].reshape(D_MODEL, 2, groups, GROUP_W), (2, 1, 3, 0)
    ).reshape(groups, 2 * GROUP_W, D_MODEL).astype(BF16)
    cos_a, sin_a = _rope_angles(jnp.arange(SEQ))
    tail0, tail1, tail2 = _tails(xp, gn[0, 1], w_kv_t, qk_gain_a[0, 1].reshape(HEAD_DIM, 1),
                                 cos_a.T, sin_a.T)

    mp = _attn_prompt(qp, kvp0, kvp1, kvp2)
    ms = _attn_sample(qs, kvs0, kvs1, kvs2, _feature_major(cache_kv_w128),
                      _feature_major(cache_kv_w512), _feature_major(cache_kv_w2048))

    xp = _ffn(xp, gn[0, 2], w_in, w_out, 0, 1, tm_p, proj=(mp, w_oa))
    xs = _ffn(xs, gn[0, 2], w_in, w_out, 0, 1, n_s, proj=(ms, w_oa))

    xp = _ffn(xp, gn[1, 0], w_in, w_out, 1, 0, tm_p)
    xs = _ffn(xs, gn[1, 0], w_in, w_out, 1, 0, n_s)

    vgain = v_gain_b[0].reshape(1, B_WIDTH)
    def lane_replicated(b):
        return jnp.broadcast_to(b[:, :, None], b.shape + (B_GROUP_DIM,))

    (xp,) = _mixb(xp, gn[1, 1], w_uv, vgain, w_sp_b[0], lane_replicated(b_sp_b[0]), w_ob,
                  tm=1024, chunk=B_CHUNK, emit_v=False)
    rep = (jnp.arange(n_s)[:, None] % DEC_SEQ == jnp.arange(DEC_SEQ)[None, :]).astype(F32)
    w_sp_s = jnp.einsum("it,gts,js->gij", rep, w_sp_b[0][:, :DEC_SEQ, :DEC_SEQ], rep,
                        precision=lax.Precision.HIGHEST)
    b_s = jnp.tile(b_sp_b[0][:, :DEC_SEQ], (1, DEC_BATCH))
    xs, v_new = _mixb(xs, gn[1, 1], w_uv, vgain, w_sp_s, lane_replicated(b_s), w_ob,
                      tm=n_s, chunk=DEC_SEQ, emit_v=True)

    xp = _ffn(xp, gn[1, 2], w_in, w_out, 1, 1, tm_p)
    xs = _ffn(xs, gn[1, 2], w_in, w_out, 1, 1, n_s)

    def rows(kv):
        return kv.reshape(1, DEC_BATCH, DEC_SEQ, 2, HEADS, HEAD_DIM)

    return (
        xp.reshape(BATCH, SEQ, D_MODEL),
        xs.reshape(DEC_BATCH, DEC_SEQ, D_MODEL),
        _position_major(tail0), _position_major(tail1), _position_major(tail2),
        rows(kvs0), rows(kvs1), rows(kvs2),
        v_new.reshape(1, DEC_BATCH, DEC_SEQ, B_WIDTH),
    )
```

```python
import functools

import jax
import jax.numpy as jnp
from jax import lax
from jax.experimental import pallas as pl
from jax.experimental.pallas import tpu as pltpu

F32 = jnp.float32
BF16 = jnp.bfloat16

D_MODEL = 1024
D_FF = 2816
HEAD_DIM = 64
HEADS = 8
GROUP_W = HEADS * HEAD_DIM
PAIR_W = 2 * HEAD_DIM
QKV_W = 3 * GROUP_W
A_GROUPS = ((128, 1), (512, 4), (2048, 16))
ROPE_THETA = 10000.0
B_WIDTH = 2048
B_GROUPS = 16
B_CHUNK = 128
EPS = 1e-6
NEG = -1e30
LOG2_E = 1.4426950408889634
LN_2 = 0.6931471805599453
Q_SCALE = HEAD_DIM ** -0.5 * LOG2_E

SEQ = 2048
BATCH = 16
DEC_BATCH = 32
DEC_SEQ = 8
PAST_LEN = 16384

V7X_VMEM_LIMIT_BYTES = 56 * 1024 * 1024
V7X_MXU_DEPTH = 256


def _log2(n):
    assert n > 0 and n & (n - 1) == 0, n
    return n.bit_length() - 1


def _dot(a, b):
    return jnp.dot(a, b, preferred_element_type=F32)


def _dot_nt(a, b):
    return lax.dot_general(a, b, (((1,), (1,)), ((), ())), preferred_element_type=F32)


def _rms(x, g):
    ms = jnp.mean(x * x, axis=-1, keepdims=True)
    return x * lax.rsqrt(ms + EPS) * g


def _rms_operand(x, g):
    r = lax.rsqrt(jnp.mean(x * x, axis=-1, keepdims=True) + EPS)
    return (x * g).astype(BF16), r


def _gelu_tanh(x):
    z2 = (2.0 * 0.7978845608028654) * (x + 0.044715 * (x * x * x))
    return x / (1.0 + jnp.exp(-z2))


def _params(n_axes):
    return pltpu.CompilerParams(
        dimension_semantics=("arbitrary",) * n_axes,
        vmem_limit_bytes=V7X_VMEM_LIMIT_BYTES,
    )


def _resident(shape, index=None):
    if index is None:
        index = (0,) * len(shape)
    return pl.BlockSpec(shape, lambda *_: index, pipeline_mode=pl.Buffered(1))


FF_CHUNK = 256


def _ffn_kernel(*refs, with_proj):
    if with_proj:
        x_ref, m_ref, wo_ref, g_ref, win_ref, wout_ref, o_ref, h_ref = refs
        x = x_ref[...] + _dot(m_ref[...].astype(BF16), wo_ref[...])
    else:
        x_ref, g_ref, win_ref, wout_ref, o_ref, h_ref = refs
        x = x_ref[...]
    xb, r = _rms_operand(x, g_ref[...])
    for c in range(D_FF // FF_CHUNK):
        lo = c * FF_CHUNK
        gate = _dot(xb, win_ref[:, lo:lo + FF_CHUNK]) * r
        up = _dot(xb, win_ref[:, D_FF + lo:D_FF + lo + FF_CHUNK]) * r
        act = gate * (1.0 / (1.0 + jnp.exp(-gate)))
        h_ref[:, lo:lo + FF_CHUNK] = (act * up).astype(BF16)
    o_ref[...] = x + 0.5 * _dot(h_ref[...], wout_ref[...])


def _ffn(x, g, w_in, w_out, layer, step, tm, proj=None):
    n = x.shape[0]
    x_spec = pl.BlockSpec((tm, D_MODEL), lambda i: (i, 0))
    in_specs = [x_spec]
    args = [x]
    if proj is not None:
        m, w_o = proj
        in_specs += [pl.BlockSpec((tm, GROUP_W), lambda i: (i, 0)), _resident((GROUP_W, D_MODEL))]
        args += [m, w_o]
    in_specs += [
        _resident((1, D_MODEL)),
        _resident((None, None, D_MODEL, 2 * D_FF), (layer, step, 0, 0)),
        _resident((None, None, D_FF, D_MODEL), (layer, step, 0, 0)),
    ]
    args += [g, w_in, w_out]
    return pl.pallas_call(
        functools.partial(_ffn_kernel, with_proj=proj is not None),
        out_shape=jax.ShapeDtypeStruct((n, D_MODEL), F32),
        grid=(n // tm,),
        in_specs=in_specs,
        out_specs=x_spec,
        scratch_shapes=[pltpu.VMEM((tm, D_FF), BF16)],
        compiler_params=_params(1),
        name="ffn",
    )(*args)


QK_BLOCK = 256
QKV_TM = 1024


def _qkv_kernel(x_ref, g_ref, w_ref, gain_ref, cos_ref, sin_ref, bd_ref, *out_refs, parts):
    xn = _rms(x_ref[...], g_ref[...]).astype(BF16)
    cos = cos_ref[...]
    sin = sin_ref[...]
    bd = bd_ref[...]

    def norm_rot(y, gain):
        ms = _dot((y * y).astype(BF16), bd)
        yn = y * lax.rsqrt(ms + EPS) * gain
        partner = jnp.concatenate(
            [pltpu.roll(yn[:, lo:lo + PAIR_W], PAIR_W // 2, 1)
             for lo in range(0, QK_BLOCK, PAIR_W)], axis=1)
        return yn * cos + partner * sin

    for part, o_ref in zip(parts, out_refs):
        if part == "q":
            for c0 in range(0, QKV_W, GROUP_W):
                y = _dot(xn, w_ref[:, c0:c0 + GROUP_W])
                for o0 in range(0, GROUP_W, QK_BLOCK):
                    o_ref[:, c0 + o0:c0 + o0 + QK_BLOCK] = norm_rot(
                        y[:, o0:o0 + QK_BLOCK], gain_ref[0:1, :])
            continue
        c0 = part * GROUP_W
        yk = _dot(xn, w_ref[:, QKV_W + c0:QKV_W + c0 + GROUP_W])
        for o0 in range(0, GROUP_W, QK_BLOCK):
            o_ref[:, o0:o0 + QK_BLOCK] = norm_rot(yk[:, o0:o0 + QK_BLOCK], gain_ref[1:2, :])
        o_ref[:, GROUP_W:] = _dot(xn, w_ref[:, 2 * QKV_W + c0:2 * QKV_W + c0 + GROUP_W])


def _qkv(x, g, w, gain, cos, sin, bd, tm, parts):
    n = x.shape[0]
    pos_blocks = cos.shape[0] // tm
    widths = [QKV_W if part == "q" else 2 * GROUP_W for part in parts]
    return pl.pallas_call(
        functools.partial(_qkv_kernel, parts=parts),
        out_shape=tuple(jax.ShapeDtypeStruct((n, wd), F32) for wd in widths),
        grid=(n // tm,),
        in_specs=[
            pl.BlockSpec((tm, D_MODEL), lambda i: (i, 0)),
            _resident((1, D_MODEL)),
            _resident((D_MODEL, 3 * QKV_W)),
            _resident((2, QK_BLOCK)),
            pl.BlockSpec((tm, QK_BLOCK), lambda i: (i % pos_blocks, 0)),
            pl.BlockSpec((tm, QK_BLOCK), lambda i: (i % pos_blocks, 0)),
            _resident((QK_BLOCK, QK_BLOCK)),
        ],
        out_specs=tuple(pl.BlockSpec((tm, wd), lambda i: (i, 0)) for wd in widths),
        compiler_params=_params(1),
        name="qkv",
    )(x, g, w, gain, cos, sin, bd)


TAIL_TM = 1024


def _tails_kernel(x_ref, g_ref, wt_ref, gain_ref, cos_ref, sin_ref, t0_ref, t1_ref, t2_ref,
                  *, tiles_per_seq):
    tm = x_ref.shape[0]
    half = HEAD_DIM // 2
    xn = _rms(x_ref[...], g_ref[...]).astype(BF16)
    gain = gain_ref[...]
    is_last_tile = (pl.program_id(0) % tiles_per_seq) == tiles_per_seq - 1

    def put_tail(g, t_ref):
        keep = t_ref.shape[-1]
        lo = tm - keep
        kv = _dot_nt(wt_ref[g], xn[lo:, :])
        cos = cos_ref[:, lo:]
        sin = sin_ref[:, lo:]
        for h in range(HEADS):
            y = kv[h * HEAD_DIM:(h + 1) * HEAD_DIM, :]
            ms = jnp.mean(y * y, axis=0, keepdims=True)
            yn = y * lax.rsqrt(ms + EPS) * gain
            x1, x2 = yn[:half], yn[half:]
            t_ref[0, h * HEAD_DIM:h * HEAD_DIM + half, :] = x1 * cos - x2 * sin
            t_ref[0, h * HEAD_DIM + half:(h + 1) * HEAD_DIM, :] = x2 * cos + x1 * sin
        t_ref[1] = kv[GROUP_W:]

    for g, t_ref in enumerate((t0_ref, t1_ref, t2_ref)):
        if A_GROUPS[g][0] >= SEQ:
            put_tail(g, t_ref)
        else:
            pl.when(is_last_tile)(functools.partial(put_tail, g, t_ref))


def _tails(x, g, wt, gain_col, cos_t, sin_t):
    n = x.shape[0]
    tm = TAIL_TM
    pos_blocks = SEQ // tm
    out_shape, out_specs = [], []
    for window, _ in A_GROUPS:
        keep = min(window, SEQ)
        assert keep <= tm or keep == SEQ
        out_shape.append(jax.ShapeDtypeStruct((n // SEQ, 2, GROUP_W, keep), F32))
        if keep > tm:
            index = lambda i: (i // pos_blocks, 0, 0, i % pos_blocks)
        else:
            index = lambda i: (i // pos_blocks, 0, 0, 0)
        out_specs.append(pl.BlockSpec((None, 2, GROUP_W, min(keep, tm)), index))
    return pl.pallas_call(
        functools.partial(_tails_kernel, tiles_per_seq=pos_blocks),
        out_shape=tuple(out_shape),
        grid=(n // tm,),
        in_specs=[
            pl.BlockSpec((tm, D_MODEL), lambda i: (i, 0)),
            _resident((1, D_MODEL)),
            _resident((len(A_GROUPS), 2 * GROUP_W, D_MODEL)),
            _resident((HEAD_DIM, 1)),
            pl.BlockSpec((HEAD_DIM // 2, tm), lambda i: (0, i % pos_blocks)),
            pl.BlockSpec((HEAD_DIM // 2, tm), lambda i: (0, i % pos_blocks)),
        ],
        out_specs=tuple(out_specs),
        compiler_params=_params(1),
        name="kv_tails",
    )(x, g, wt, gain_col, cos_t, sin_t)


BLK = 128
SPAN = 4


def _attn_kernel(q0_ref, q1_ref, q2_ref, k0_ref, v0_ref, k1_ref, v1_ref, k2_ref, v2_ref,
                 o_ref, og0, og1, og2, lg0, lg1, lg2, kmask_ref, qsel_ref):
    lane = lax.broadcasted_iota(jnp.int32, (BLK, PAIR_W), 1)
    head_a = lane < HEAD_DIM
    lane_ov = lax.broadcasted_iota(jnp.int32, (BLK, 2 * PAIR_W), 1)
    head_a2 = (lane_ov & (PAIR_W - 1)) < HEAD_DIM
    row2 = lax.broadcasted_iota(jnp.int32, (2 * BLK, PAIR_W), 0)
    lane2 = lax.broadcasted_iota(jnp.int32, (2 * BLK, PAIR_W), 1)
    q_keep = (row2 >> _log2(BLK)) == ((lane2 >> _log2(HEAD_DIM // 2)) & 1)
    key = lax.broadcasted_iota(jnp.int32, (2 * BLK, BLK), 0)
    qi = lax.broadcasted_iota(jnp.int32, (2 * BLK, BLK), 1)
    jk = key & (BLK - 1)
    valid = ((key < BLK) & (jk >= qi)) | ((key >= BLK) & (jk <= qi))
    kmask_ref[...] = jnp.where(valid, 0.0, NEG).astype(BF16)
    qsel_ref[...] = jnp.where(jk == qi, 1.0, 0.0).astype(BF16)

    def one_block(q, kb, vb, kmask):
        q2 = jnp.where(q_keep, jnp.concatenate([q, q], axis=0), 0.0).astype(BF16)
        s = _dot_nt(jnp.concatenate([q2, qsel_ref[...]], axis=1),
                    jnp.concatenate([kb, kmask], axis=1))
        m = jnp.max(s, axis=1, keepdims=True)
        p = jnp.exp2(s - m).astype(BF16)
        ones = jnp.ones((vb.shape[0], PAIR_W), BF16)
        pv = _dot(p, jnp.concatenate([vb, ones], axis=1))
        pv = jnp.where(head_a2, pv[:BLK], pv[BLK:])
        m = jnp.where(head_a, m[:BLK], m[BLK:])
        l = pv[:, PAIR_W:]
        return pv[:, :PAIR_W] / l, m * LN_2 + jnp.log(l)

    def span(refs, dil, r, n0, count, has_prev):
        q_ref, k_ref, v_ref, og_ref, lg_ref = refs

        def rows(n):
            return pl.ds(r + (n0 + n) * (dil * BLK), BLK, stride=dil)

        first = -1 if has_prev else 0
        kb = {n: k_ref[rows(n), :].astype(BF16) for n in range(first, count)}
        vb = {n: v_ref[rows(n), :].astype(BF16) for n in range(first, count)}
        for n in range(count):
            q = q_ref[rows(n), :]
            if n - 1 in kb:
                kcat = jnp.concatenate([kb[n - 1], kb[n]], axis=0)
                vcat = jnp.concatenate([vb[n - 1], vb[n]], axis=0)
                o, lse = one_block(q, kcat, vcat, kmask_ref[...])
            else:
                o, lse = one_block(q, kb[n], vb[n], kmask_ref[BLK:, :])
            og_ref[rows(n), :] = o
            lg_ref[rows(n), :] = lse

    refs0 = (q0_ref, k0_ref, v0_ref, og0, lg0)
    refs1 = (q1_ref, k1_ref, v1_ref, og1, lg1)
    refs2 = (q2_ref, k2_ref, v2_ref, og2, lg2)
    dil0, dil1, dil2 = (d for _, d in A_GROUPS)
    blocks0, blocks1, blocks2 = (SEQ // (d * BLK) for _, d in A_GROUPS)
    assert blocks0 % SPAN == 0 and blocks1 == SPAN and blocks2 == 1 and dil2 % SPAN == 0

    for it in range(blocks0 // SPAN):
        span(refs0, dil0, 0, it * SPAN, SPAN, it > 0)
    for r in range(dil1):
        span(refs1, dil1, r, 0, SPAN, False)
    for r in range(dil2):
        span(refs2, dil2, r, 0, 1, False)

    def merge(i, carry):
        rows = pl.ds(pl.multiple_of(i * BLK, BLK), BLK)
        l0, l1, l2 = lg0[rows, :], lg1[rows, :], lg2[rows, :]
        mx = jnp.maximum(jnp.maximum(l0, l1), l2)
        e0, e1, e2 = jnp.exp(l0 - mx), jnp.exp(l1 - mx), jnp.exp(l2 - mx)
        num = e0 * og0[rows, :] + e1 * og1[rows, :] + e2 * og2[rows, :]
        o_ref[rows, :] = (num / (e0 + e1 + e2)).astype(o_ref.dtype)
        return carry

    lax.fori_loop(0, SEQ // BLK, merge, 0, unroll=4)


def _attn_prompt(q, kv0, kv1, kv2):
    n = q.shape[0]
    pairs = GROUP_W // PAIR_W
    blk = (SEQ, PAIR_W)
    q_specs = [pl.BlockSpec(blk, functools.partial(lambda b, hp, g: (b, g * pairs + hp), g=g))
               for g in range(3)]
    k_spec = pl.BlockSpec(blk, lambda b, hp: (b, hp))
    v_spec = pl.BlockSpec(blk, lambda b, hp: (b, pairs + hp))
    scratch = [pltpu.VMEM(blk, F32) for _ in range(6)] + [pltpu.VMEM((2 * BLK, BLK), BF16)] * 2
    return pl.pallas_call(
        _attn_kernel,
        out_shape=jax.ShapeDtypeStruct((n, GROUP_W), BF16),
        grid=(n // SEQ, pairs),
        in_specs=q_specs + [k_spec, v_spec, k_spec, v_spec, k_spec, v_spec],
        out_specs=pl.BlockSpec(blk, lambda b, hp: (b, hp)),
        scratch_shapes=scratch,
        compiler_params=_params(2),
        name="attn_prompt",
    )(q, q, q, kv0, kv0, kv1, kv1, kv2, kv2)


NEW_PAD = 128
QROWS = HEADS * DEC_SEQ


def _attn_sample_kernel(q_ref, n0_ref, n1_ref, n2_ref, c0_ref, c1_ref, c2_ref, o_ref):
    new_refs = (n0_ref, n1_ref, n2_ref)
    cache_refs = (c0_ref, c1_ref, c2_ref)
    rowq = lax.broadcasted_iota(jnp.int32, (QROWS, GROUP_W), 0)
    laneq = lax.broadcasted_iota(jnp.int32, (QROWS, GROUP_W), 1)
    own_head = (rowq >> _log2(DEC_SEQ)) == (laneq >> _log2(HEAD_DIM))
    outs, lses = [], []
    for g, (window, dil) in enumerate(A_GROUPS):
        q = q_ref[:, g * GROUP_W:(g + 1) * GROUP_W]
        qbd = jnp.where(own_head, jnp.concatenate([q] * HEADS, axis=0), 0.0).astype(BF16)
        k_old = cache_refs[g][0].astype(BF16)
        v_old = cache_refs[g][1].astype(BF16)
        new = jnp.concatenate(
            [new_refs[g][...], jnp.zeros((NEW_PAD - DEC_SEQ, 2 * GROUP_W), F32)], axis=0)
        k_new = new[:, :GROUP_W].astype(BF16)
        v_new = new[:, GROUP_W:].astype(BF16)

        key = lax.broadcasted_iota(jnp.int32, (QROWS, window), 1)
        j = lax.broadcasted_iota(jnp.int32, (QROWS, window), 0) & (DEC_SEQ - 1)
        back = window + j - key
        valid_old = ((back & (dil - 1)) == 0) & (key >= j)
        jn = lax.broadcasted_iota(jnp.int32, (QROWS, NEW_PAD), 1)
        jq = lax.broadcasted_iota(jnp.int32, (QROWS, NEW_PAD), 0) & (DEC_SEQ - 1)
        valid_new = (jn <= jq) & (((jq - jn) & (dil - 1)) == 0)

        s_old = jnp.where(valid_old, _dot(qbd, k_old), NEG)
        s_new = jnp.where(valid_new, _dot_nt(qbd, k_new), NEG)
        m = jnp.maximum(jnp.max(s_old, axis=1, keepdims=True),
                        jnp.max(s_new, axis=1, keepdims=True))
        p_old = jnp.exp2(s_old - m)
        p_new = jnp.exp2(s_new - m)
        l = jnp.sum(p_old, axis=1, keepdims=True) + jnp.sum(p_new, axis=1, keepdims=True)
        o = (_dot_nt(p_old.astype(BF16), v_old) + _dot(p_new.astype(BF16), v_new)) / l
        lse = m * LN_2 + jnp.log(l)
        om = jnp.where(own_head, o, 0.0)
        lm = jnp.where(own_head, lse, 0.0)
        og, lg = om[0:DEC_SEQ], lm[0:DEC_SEQ]
        for h in range(1, HEADS):
            og = og + om[h * DEC_SEQ:(h + 1) * DEC_SEQ]
            lg = lg + lm[h * DEC_SEQ:(h + 1) * DEC_SEQ]
        outs.append(og)
        lses.append(lg)
    mx = jnp.maximum(jnp.maximum(lses[0], lses[1]), lses[2])
    e = [jnp.exp(x - mx) for x in lses]
    num = e[0] * outs[0] + e[1] * outs[1] + e[2] * outs[2]
    o_ref[...] = num / (e[0] + e[1] + e[2])


def _attn_sample(q, new0, new1, new2, c0, c1, c2):
    row_spec = lambda w: pl.BlockSpec((DEC_SEQ, w), lambda b: (b, 0))
    cache_spec = lambda c: pl.BlockSpec((None,) + c.shape[1:], lambda b: (b, 0, 0, 0))
    return pl.pallas_call(
        _attn_sample_kernel,
        out_shape=jax.ShapeDtypeStruct((DEC_BATCH * DEC_SEQ, GROUP_W), F32),
        grid=(DEC_BATCH,),
        in_specs=[
            row_spec(QKV_W), row_spec(2 * GROUP_W), row_spec(2 * GROUP_W), row_spec(2 * GROUP_W),
            cache_spec(c0), cache_spec(c1), cache_spec(c2),
        ],
        out_specs=row_spec(GROUP_W),
        compiler_params=_params(1),
        name="attn_sample",
    )(q, new0, new1, new2, c0, c1, c2)


UV_CHUNK = 512
B_GROUP_DIM = B_WIDTH // B_GROUPS


def _mixb_kernel(x_ref, g_ref, wuv_ref, vgain_ref, wsp_ref, bias_ref, wo_ref, *rest,
                 chunk, emit_v):
    if emit_v:
        o_ref, vout_ref, v_ref, h_ref = rest
    else:
        o_ref, v_ref, h_ref = rest
    tm = x_ref.shape[0]
    r = wsp_ref.shape[1]
    nsub = tm // r
    x = x_ref[...]
    xn = _rms(x, g_ref[...]).astype(BF16)
    ssq = jnp.zeros((tm, B_GROUP_DIM), F32)
    for c in range(B_WIDTH // UV_CHUNK):
        lo = c * UV_CHUNK
        v_raw = _dot(xn, wuv_ref[:, B_WIDTH + lo:B_WIDTH + lo + UV_CHUNK])
        for o in range(0, UV_CHUNK, B_GROUP_DIM):
            vv = _gelu_tanh(v_raw[:, o:o + B_GROUP_DIM])
            v_ref[:, lo + o:lo + o + B_GROUP_DIM] = vv
            ssq = ssq + vv * vv
    inv = lax.rsqrt(jnp.sum(ssq, axis=-1, keepdims=True) * (1.0 / B_WIDTH) + EPS)
    row = lax.broadcasted_iota(jnp.int32, (r, r), 0)
    col = lax.broadcasted_iota(jnp.int32, (r, r), 1)
    shift = _log2(chunk)
    keep = ((row >> shift) == (col >> shift)) & ((col & (chunk - 1)) <= (row & (chunk - 1)))
    groups_per_chunk = UV_CHUNK // B_GROUP_DIM
    stack = max(1, V7X_MXU_DEPTH // r)
    for c in range(B_WIDTH // UV_CHUNK):
        u_raw = _dot(xn, wuv_ref[:, c * UV_CHUNK:(c + 1) * UV_CHUNK])
        for gi0 in range(0, groups_per_chunk, stack):
            gis = range(gi0, gi0 + stack)
            vbs, ws = [], []
            for gi in gis:
                g = c * groups_per_chunk + gi
                lanes = slice(g * B_GROUP_DIM, (g + 1) * B_GROUP_DIM)
                vg = v_ref[:, lanes] * inv * vgain_ref[:, lanes]
                if emit_v:
                    vout_ref[:, lanes] = vg
                vb = vg.astype(BF16)
                if nsub > 1:
                    vb = jnp.concatenate([vb[s * r:(s + 1) * r] for s in range(nsub)], axis=1)
                vbs.append(vb)
                ws.append(jnp.where(keep, wsp_ref[g], 0.0).astype(BF16))
            if stack == 1:
                mixed_all = _dot(ws[0], vbs[0])
            else:
                zero = jnp.zeros((r, r), BF16)
                w_bd = jnp.concatenate(
                    [jnp.concatenate([ws[i] if i == j else zero for j in range(stack)], axis=1)
                     for i in range(stack)], axis=0)
                mixed_all = _dot(w_bd, jnp.concatenate(vbs, axis=0))
            for i, gi in enumerate(gis):
                g = c * groups_per_chunk + gi
                lanes = slice(g * B_GROUP_DIM, (g + 1) * B_GROUP_DIM)
                mixed = mixed_all[i * r:(i + 1) * r]
                bias = bias_ref[g]
                ug = _gelu_tanh(u_raw[:, gi * B_GROUP_DIM:(gi + 1) * B_GROUP_DIM])
                for s in range(nsub):
                    rows = slice(s * r, (s + 1) * r)
                    h_ref[rows, lanes] = (
                        ug[rows] * (mixed[:, s * B_GROUP_DIM:(s + 1) * B_GROUP_DIM] + bias)
                    ).astype(BF16)
    o_ref[...] = x + _dot(h_ref[...], wo_ref[...])


def _mixb(x, g, w_uv, v_gain, w_sp, b_t, w_o, tm, chunk, emit_v):
    n = x.shape[0]
    r = w_sp.shape[1]
    x_spec = pl.BlockSpec((tm, D_MODEL), lambda i: (i, 0))
    out_shape = [jax.ShapeDtypeStruct((n, D_MODEL), F32)]
    out_specs = [x_spec]
    if emit_v:
        out_shape.append(jax.ShapeDtypeStruct((n, B_WIDTH), F32))
        out_specs.append(pl.BlockSpec((tm, B_WIDTH), lambda i: (i, 0)))
    return pl.pallas_call(
        functools.partial(_mixb_kernel, chunk=chunk, emit_v=emit_v),
        out_shape=tuple(out_shape),
        grid=(n // tm,),
        in_specs=[
            x_spec,
            _resident((1, D_MODEL)),
            _resident((D_MODEL, 2 * B_WIDTH)),
            _resident((1, B_WIDTH)),
            _resident((B_GROUPS, r, r)),
            _resident((B_GROUPS, r, B_GROUP_DIM)),
            _resident((B_WIDTH, D_MODEL)),
        ],
        out_specs=tuple(out_specs),
        scratch_shapes=[
            pltpu.VMEM((tm, B_WIDTH), F32),
            pltpu.VMEM((tm, B_WIDTH), BF16),
        ],
        compiler_params=_params(1),
        name="spatial_gating",
    )(x, g, w_uv, v_gain, w_sp, b_t, w_o)


def _rope_angles(pos):
    half = HEAD_DIM // 2
    inv = ROPE_THETA ** (-jnp.arange(half, dtype=F32) / half)
    ang = pos.astype(F32)[:, None] * inv[None, :]
    return jnp.cos(ang), jnp.sin(ang)


def _pair_order(width):
    half = HEAD_DIM // 2
    n = jnp.arange(width)
    lane = n % PAIR_W
    head, hi = (lane // half) % 2, lane // HEAD_DIM
    return (n // PAIR_W) * PAIR_W + head * HEAD_DIM + hi * half + lane % half


def _rope_tables(pos):
    cos, sin = _rope_angles(pos)
    cos_t = jnp.tile(cos, (1, QK_BLOCK // (HEAD_DIM // 2)))
    sin_t = jnp.tile(jnp.concatenate([-sin, -sin, sin, sin], axis=-1), (1, QK_BLOCK // PAIR_W))
    return cos_t, sin_t


def _feature_major(cache):
    _, b, length, two, h, hd = cache.shape
    return jnp.transpose(cache[0], (0, 2, 3, 4, 1)).reshape(b, two, h * hd, length)


def _position_major(tail):
    b, two, _, length = tail.shape
    t = tail.reshape(b, two, HEADS, HEAD_DIM, length)
    return jnp.transpose(t, (0, 4, 1, 2, 3))[None]


def kernel(x_prompt, x_sample, cache_kv_w128, cache_kv_w512, cache_kv_w2048, norm_g,
           w_ffn_in, w_ffn_out, w_qkv_a, qk_gain_a, w_o_a, w_uv_b, v_gain_b, w_sp_b, b_sp_b,
           w_o_b):
    xp = x_prompt.reshape(BATCH * SEQ, D_MODEL)
    xs = x_sample.reshape(DEC_BATCH * DEC_SEQ, D_MODEL)
    n_s = DEC_BATCH * DEC_SEQ
    tm_p = 1024

    w_in = w_ffn_in.astype(BF16)
    w_out = w_ffn_out.astype(BF16)
    order = _pair_order(QKV_W)
    w_qkv = jnp.concatenate(
        [w_qkv_a[0][:, :QKV_W][:, order], w_qkv_a[0][:, QKV_W:2 * QKV_W][:, order],
         w_qkv_a[0][:, 2 * QKV_W:]], axis=1).astype(BF16)
    w_oa = w_o_a[0].astype(BF16)
    w_uv = w_uv_b[0].astype(BF16)
    w_ob = w_o_b[0].astype(BF16)
    gn = norm_g.reshape(2, 3, 1, D_MODEL)

    natural = order[:QK_BLOCK]
    head_id = natural // HEAD_DIM
    bd = jnp.where(head_id[:, None] == head_id[None, :], 1.0 / HEAD_DIM, 0.0).astype(BF16)
    gain = qk_gain_a[0][:, natural % HEAD_DIM] * jnp.array([[Q_SCALE], [1.0]], F32)
    cos_p, sin_p = _rope_tables(jnp.arange(SEQ))
    cos_s, sin_s = _rope_tables(PAST_LEN + jnp.arange(DEC_SEQ))
    cos_s = jnp.tile(cos_s, (DEC_BATCH, 1))
    sin_s = jnp.tile(sin_s, (DEC_BATCH, 1))

    xp = _ffn(xp, gn[0, 0], w_in, w_out, 0, 0, tm_p)
    xs = _ffn(xs, gn[0, 0], w_in, w_out, 0, 0, n_s)

    qp, kvp0 = _qkv(xp, gn[0, 1], w_qkv, gain, cos_p, sin_p, bd, QKV_TM, ("q", 0))
    kvp1, kvp2 = _qkv(xp, gn[0, 1], w_qkv, gain, cos_p, sin_p, bd, QKV_TM, (1, 2))
    qs, kvs0, kvs1, kvs2 = _qkv(xs, gn[0, 1], w_qkv, gain, cos_s, sin_s, bd, n_s, ("q", 0, 1, 2))
    qs = qs[:, order]
    kvs0, kvs1, kvs2 = (
        jnp.concatenate([kv[:, :GROUP_W][:, order[:GROUP_W]], kv[:, GROUP_W:]], axis=1)
        for kv in (kvs0, kvs1, kvs2))

    groups = len(A_GROUPS)
    w_kv_t = jnp.transpose(
        w_qkv_a[0][:, QKV_W:].reshape(D_MODEL, 2, groups, GROUP_W), (2, 1, 3, 0)
    ).reshape(groups, 2 * GROUP_W, D_MODEL).astype(BF16)
    cos_a, sin_a = _rope_angles(jnp.arange(SEQ))
    tail0, tail1, tail2 = _tails(xp, gn[0, 1], w_kv_t, qk_gain_a[0, 1].reshape(HEAD_DIM, 1),
                                 cos_a.T, sin_a.T)

    mp = _attn_prompt(qp, kvp0, kvp1, kvp2)
    ms = _attn_sample(qs, kvs0, kvs1, kvs2, _feature_major(cache_kv_w128),
                      _feature_major(cache_kv_w512), _feature_major(cache_kv_w2048))

    xp = _ffn(xp, gn[0, 2], w_in, w_out, 0, 1, tm_p, proj=(mp, w_oa))
    xs = _ffn(xs, gn[0, 2], w_in, w_out, 0, 1, n_s, proj=(ms, w_oa))

    xp = _ffn(xp, gn[1, 0], w_in, w_out, 1, 0, tm_p)
    xs = _ffn(xs, gn[1, 0], w_in, w_out, 1, 0, n_s)

    vgain = v_gain_b[0].reshape(1, B_WIDTH)
    def lane_replicated(b):
        return jnp.broadcast_to(b[:, :, None], b.shape + (B_GROUP_DIM,))

    (xp,) = _mixb(xp, gn[1, 1], w_uv, vgain, w_sp_b[0], lane_replicated(b_sp_b[0]), w_ob,
                  tm=1024, chunk=B_CHUNK, emit_v=False)
    rep = (jnp.arange(n_s)[:, None] % DEC_SEQ == jnp.arange(DEC_SEQ)[None, :]).astype(F32)
    w_sp_s = jnp.einsum("it,gts,js->gij", rep, w_sp_b[0][:, :DEC_SEQ, :DEC_SEQ], rep,
                        precision=lax.Precision.HIGHEST)
    b_s = jnp.tile(b_sp_b[0][:, :DEC_SEQ], (1, DEC_BATCH))
    xs, v_new = _mixb(xs, gn[1, 1], w_uv, vgain, w_sp_s, lane_replicated(b_s), w_ob,
                      tm=n_s, chunk=DEC_SEQ, emit_v=True)

    xp = _ffn(xp, gn[1, 2], w_in, w_out, 1, 1, tm_p)
    xs = _ffn(xs, gn[1, 2], w_in, w_out, 1, 1, n_s)

    def rows(kv):
        return kv.reshape(1, DEC_BATCH, DEC_SEQ, 2, HEADS, HEAD_DIM)

    return (
        xp.reshape(BATCH, SEQ, D_MODEL),
        xs.reshape(DEC_BATCH, DEC_SEQ, D_MODEL),
        _position_major(tail0), _position_major(tail1), _position_major(tail2),
        rows(kvs0), rows(kvs1), rows(kvs2),
        v_new.reshape(1, DEC_BATCH, DEC_SEQ, B_WIDTH),
    )
```
